```python
import math
import jax, jax.numpy as jnp
from jax import lax
import numpy as np

D_MODEL = 2048
BATCH = 4
SEQ = 4096
DEPTH = 2

N_MEM = 256
EPS = 1e-6
ROPE_THETA = 10000.0
Q_BLOCK = 128

MLA_HEADS = 8
MLA_Q_RANK = 512
MLA_KV_RANK = 512
MLA_NOPE = 128
MLA_ROPE = 64
MLA_V = 128
MLA_QK = MLA_NOPE + MLA_ROPE

SSD_HEADS = 16
SSD_HEAD_DIM = 64
SSD_INNER = SSD_HEADS * SSD_HEAD_DIM
SSD_GROUPS = 4
SSD_STATE = 128
SSD_CONV = 4
SSD_CHUNK = 128
SSD_CONV_DIM = SSD_INNER + 2 * SSD_GROUPS * SSD_STATE

EVEN_SPLITS = [MLA_Q_RANK, MLA_KV_RANK, MLA_ROPE, SSD_INNER, SSD_CONV_DIM, SSD_HEADS]
EVEN_IN = sum(EVEN_SPLITS)
EVEN_MIX = MLA_HEADS * MLA_V + SSD_INNER

DIFF_HEADS = 8
DIFF_HEAD_DIM = D_MODEL // DIFF_HEADS // 2
DIFF_V = 2 * DIFF_HEAD_DIM
DIFF_MIX = DIFF_HEADS * DIFF_V

MEM_HEADS = 4
MEM_HEAD_DIM = 128
MEM_WIDTH = MEM_HEADS * MEM_HEAD_DIM

PEER_HEADS = 8
PEER_KEYS = 128
PEER_EXPERTS = PEER_KEYS * PEER_KEYS
PEER_TOPK = 16
PEER_DKEY = 256
PEER_HALF = PEER_DKEY // 2
PEER_TOKEN_BLOCK = 128

N_EVEN = (DEPTH + 1) // 2
N_ODD = DEPTH // 2

kernel_name = "hybrid_mla_ssd_diffattn_peer_block"


def rms_norm(x, g):
    xf = x.astype(jnp.float32)
    y = xf * lax.rsqrt(jnp.mean(xf * xf, axis=-1, keepdims=True) + EPS)
    return (y * g.astype(jnp.float32)).astype(x.dtype)


def rope(x, positions):
    d = x.shape[-1]
    half = d // 2
    inv = 1.0 / (ROPE_THETA ** (jnp.arange(half, dtype=jnp.float32) / half))
    ang = positions.astype(jnp.float32)[..., None] * inv
    shape = ang.shape[:2] + (1,) * (x.ndim - 3) + (half,)
    cos = jnp.cos(ang).reshape(shape)
    sin = jnp.sin(ang).reshape(shape)
    x1 = x[..., :half].astype(jnp.float32)
    x2 = x[..., half:].astype(jnp.float32)
    return jnp.concatenate([x1 * cos - x2 * sin, x2 * cos + x1 * sin], axis=-1).astype(x.dtype)


def causal_block_attention(q, k, v, scale):
    b, s, h, m, dk = q.shape
    nb = s // Q_BLOCK
    qb = jnp.moveaxis(q.reshape(b, nb, Q_BLOCK, h, m, dk), 1, 0)
    kpos = jnp.arange(s)

    def block(args):
        qi, i = args
        sc = jnp.einsum('bqhmd,bkhmd->bhmqk', qi, k).astype(jnp.float32) * scale
        qpos = i * Q_BLOCK + jnp.arange(Q_BLOCK)
        sc = jnp.where(kpos[None, :] <= qpos[:, None], sc, -jnp.inf)
        p = jax.nn.softmax(sc, axis=-1).astype(v.dtype)
        return jnp.einsum('bhmqk,bkhd->bqhmd', p, v)

    out = lax.map(block, (qb, jnp.arange(nb)))
    return jnp.moveaxis(out, 0, 1).reshape(b, s, h, m, v.shape[-1])


def causal_depthwise_conv(x, w, bias):
    kw, c = w.shape
    out = lax.conv_general_dilated(x, w[:, None, :].astype(x.dtype), window_strides=(1,),
                                   padding=[(kw - 1, 0)],
                                   dimension_numbers=('NWC', 'WIO', 'NWC'),
                                   feature_group_count=c)
    return out + bias.astype(x.dtype)


def segsum(a):
    cs = jnp.cumsum(a, axis=-1)
    diff = cs[..., :, None] - cs[..., None, :]
    t = a.shape[-1]
    mask = jnp.tril(jnp.ones((t, t), dtype=bool))
    return jnp.where(mask, diff, -jnp.inf)


def ssd_chunked(x, dt, a, b_in, c_in):
    bsz, s, h, p = x.shape
    g, n = b_in.shape[-2:]
    hg = h // g
    nc = s // SSD_CHUNK
    l = SSD_CHUNK
    xd = (x * dt[..., None]).reshape(bsz, nc, l, g, hg, p)
    adt = jnp.transpose((dt * a).reshape(bsz, nc, l, g, hg), (0, 3, 4, 1, 2))
    bc = b_in.reshape(bsz, nc, l, g, n)
    cc = c_in.reshape(bsz, nc, l, g, n)
    a_cs = jnp.cumsum(adt, axis=-1)
    decay_in = jnp.exp(segsum(adt))
    cb = jnp.einsum('bclgn,bcsgn->bgcls', cc, bc)
    y_diag = jnp.einsum('bgcls,bghcls,bcsghp->bclghp', cb, decay_in, xd)
    decay_states = jnp.exp(a_cs[..., -1:] - a_cs)
    states = jnp.einsum('bclgn,bghcl,bclghp->bcghpn', bc, decay_states, xd)
    chunk_decay = jnp.exp(a_cs[..., -1])

    def step(h_prev, inp):
        st, dec = inp
        return h_prev * dec[..., None, None] + st, h_prev

    h0 = jnp.zeros((bsz, g, hg, p, n), dtype=x.dtype)
    _, prev = lax.scan(step, h0, (jnp.moveaxis(states, 1, 0), jnp.moveaxis(chunk_decay, 3, 0)))
    prev = jnp.moveaxis(prev, 0, 1)
    y_off = jnp.einsum('bclgn,bcghpn,bghcl->bclghp', cc, prev, jnp.exp(a_cs))
    return (y_diag + y_off).reshape(bsz, s, h, p)


def even_mixer(xn, positions, w_in, q_norm, kv_norm, w_uq, w_ukv, q_gain, k_gain,
               conv_w, conv_b, dt_bias, a_log, d_skip, ssd_norm, w_out):
    bsz, s, _ = xn.shape
    proj = xn @ w_in
    c_q, c_kv, k_r, z, xbc, dt = jnp.split(proj, list(np.cumsum(EVEN_SPLITS)[:-1]), axis=-1)
    q = (rms_norm(c_q, q_norm) @ w_uq).reshape(bsz, s, MLA_HEADS, MLA_QK)
    kv = (rms_norm(c_kv, kv_norm) @ w_ukv).reshape(bsz, s, MLA_HEADS, MLA_NOPE + MLA_V)
    k_nope, v = kv[..., :MLA_NOPE], kv[..., MLA_NOPE:]
    k = jnp.concatenate([k_nope, jnp.broadcast_to(k_r[:, :, None, :], (bsz, s, MLA_HEADS, MLA_ROPE))], axis=-1)
    q = rms_norm(q, q_gain)
    k = rms_norm(k, k_gain)
    q = jnp.concatenate([q[..., :MLA_NOPE], rope(q[..., MLA_NOPE:], positions)], axis=-1)
    k = jnp.concatenate([k[..., :MLA_NOPE], rope(k[..., MLA_NOPE:], positions)], axis=-1)
    attn = causal_block_attention(q[:, :, :, None], k[:, :, :, None], v, MLA_QK ** -0.5)[:, :, :, 0]
    mla_out = attn.reshape(bsz, s, MLA_HEADS * MLA_V)
    xbc = jax.nn.silu(causal_depthwise_conv(xbc, conv_w, conv_b))
    xs, b_ssm, c_ssm = jnp.split(xbc, [SSD_INNER, SSD_INNER + SSD_GROUPS * SSD_STATE], axis=-1)
    xs = xs.reshape(bsz, s, SSD_HEADS, SSD_HEAD_DIM).astype(jnp.float32)
    dt = jax.nn.softplus(dt.astype(jnp.float32) + dt_bias.astype(jnp.float32))
    a = -jnp.exp(a_log.astype(jnp.float32))
    y = ssd_chunked(xs, dt, a,
                    b_ssm.reshape(bsz, s, SSD_GROUPS, SSD_STATE).astype(jnp.float32),
                    c_ssm.reshape(bsz, s, SSD_GROUPS, SSD_STATE).astype(jnp.float32))
    y = y + d_skip.astype(jnp.float32)[:, None] * xs
    y = y.reshape(bsz, s, SSD_INNER) * jax.nn.silu(z.astype(jnp.float32))
    y = rms_norm(y.reshape(bsz, s, SSD_GROUPS, SSD_INNER // SSD_GROUPS),
                 ssd_norm.reshape(SSD_GROUPS, SSD_INNER // SSD_GROUPS))
    y = y.reshape(bsz, s, SSD_INNER).astype(xn.dtype)
    return jnp.concatenate([mla_out, y], axis=-1) @ w_out


def odd_mixer(xn, positions, layer, w_qkv, q_gain, k_gain, lam_q1, lam_k1, lam_q2, lam_k2, subln, w_o):
    bsz, s, _ = xn.shape
    q, k, v = jnp.split(xn @ w_qkv, 3, axis=-1)
    q = q.reshape(bsz, s, DIFF_HEADS, 2, DIFF_HEAD_DIM)
    k = k.reshape(bsz, s, DIFF_HEADS, 2, DIFF_HEAD_DIM)
    v = v.reshape(bsz, s, DIFF_HEADS, DIFF_V)
    q = rope(rms_norm(q, q_gain), positions)
    k = rope(rms_norm(k, k_gain), positions)
    out = causal_block_attention(q, k, v, DIFF_HEAD_DIM ** -0.5)
    lam_init = 0.8 - 0.6 * math.exp(-0.3 * layer)
    lam = (jnp.exp(jnp.sum(lam_q1.astype(jnp.float32) * lam_k1.astype(jnp.float32)))
           - jnp.exp(jnp.sum(lam_q2.astype(jnp.float32) * lam_k2.astype(jnp.float32))) + lam_init)
    attn = out[:, :, :, 0] - lam.astype(out.dtype) * out[:, :, :, 1]
    attn = rms_norm(attn, subln) * (1.0 - lam_init)
    return attn.reshape(bsz, s, DIFF_MIX) @ w_o


def memory_attention(xn, mem_n, w_q, w_kv, q_gain, k_gain, w_o):
    bsz, s, _ = xn.shape
    m = mem_n.shape[1]
    q = rms_norm((xn @ w_q).reshape(bsz, s, MEM_HEADS, MEM_HEAD_DIM), q_gain)
    mk, mv = jnp.split(mem_n @ w_kv, 2, axis=-1)
    mk = rms_norm(mk.reshape(bsz, m, MEM_HEADS, MEM_HEAD_DIM), k_gain)
    mv = mv.reshape(bsz, m, MEM_HEADS, MEM_HEAD_DIM)
    sc = jnp.einsum('bqhd,bkhd->bhqk', q, mk).astype(jnp.float32) * (MEM_HEAD_DIM ** -0.5)
    p = jax.nn.softmax(sc, axis=-1).astype(mv.dtype)
    o = jnp.einsum('bhqk,bkhd->bqhd', p, mv).reshape(bsz, s, MEM_WIDTH)
    return o @ w_o


def peer(xn, w_q, sub_keys, u_tab, v_tab):
    bsz, s, d = xn.shape
    q = (xn @ w_q).reshape(bsz, s, PEER_HEADS, 2, PEER_HALF)
    sc = jnp.einsum('bshpk,hpnk->bshpn', q, sub_keys).astype(jnp.float32)
    top_s, top_i = lax.top_k(sc, PEER_TOPK)
    cand_s = top_s[..., 0, :, None] + top_s[..., 1, None, :]
    cand_i = top_i[..., 0, :, None] * PEER_KEYS + top_i[..., 1, None, :]
    kk = PEER_TOPK * PEER_TOPK
    best_s, best_pos = lax.top_k(cand_s.reshape(bsz, s, PEER_HEADS, kk), PEER_TOPK)
    experts = jnp.take_along_axis(cand_i.reshape(bsz, s, PEER_HEADS, kk), best_pos, axis=-1)
    gates = jax.nn.softmax(best_s, axis=-1).astype(xn.dtype)
    nblk = (bsz * s) // PEER_TOKEN_BLOCK
    hk = PEER_HEADS * PEER_TOPK
    xt = xn.reshape(nblk, PEER_TOKEN_BLOCK, d)
    et = experts.reshape(nblk, PEER_TOKEN_BLOCK, hk)
    gt = gates.reshape(nblk, PEER_TOKEN_BLOCK, hk)

    def block(args):
        xb, eb, gb = args
        u = u_tab[eb]
        act = jax.nn.gelu(jnp.einsum('td,tkd->tk', xb, u), approximate=False)
        return jnp.einsum('tk,tkd->td', act * gb, v_tab[eb])

    out = lax.map(block, (xt, et, gt))
    return out.reshape(bsz, s, d)


def setup_inputs(seed: int = 0) -> dict:
    key = jax.random.key(seed)
    ks = iter(jax.random.split(key, 64))

    def nrm(shape, scale):
        return jax.random.normal(next(ks), shape, jnp.float32) * scale

    def gain(shape):
        return 1.0 + nrm(shape, 0.02)

    D = D_MODEL
    inp = {}
    inp['x'] = nrm((BATCH, SEQ, D), 1.0)
    inp['mem'] = nrm((BATCH, N_MEM, D), 1.0)
    inp['positions'] = jnp.broadcast_to(jnp.arange(SEQ, dtype=jnp.int32), (BATCH, SEQ))
    inp['norm_mix'] = gain((DEPTH, D))
    inp['norm_mem'] = gain((DEPTH, D))
    inp['norm_mem_kv'] = gain((DEPTH, D))
    inp['norm_ffn'] = gain((DEPTH, D))
    inp['mem_w_q'] = nrm((DEPTH, D, MEM_WIDTH), D ** -0.5)
    inp['mem_w_kv'] = nrm((DEPTH, D, 2 * MEM_WIDTH), D ** -0.5)
    inp['mem_q_gain'] = gain((DEPTH, MEM_HEAD_DIM))
    inp['mem_k_gain'] = gain((DEPTH, MEM_HEAD_DIM))
    inp['mem_w_o'] = nrm((DEPTH, MEM_WIDTH, D), MEM_WIDTH ** -0.5)
    inp['peer_w_q'] = nrm((DEPTH, D, PEER_HEADS * PEER_DKEY), D ** -0.5)
    inp['peer_sub_keys'] = nrm((DEPTH, PEER_HEADS, 2, PEER_KEYS, PEER_HALF), PEER_HALF ** -0.5)
    inp['peer_u'] = nrm((DEPTH, PEER_EXPERTS, D), D ** -0.5)
    inp['peer_v'] = nrm((DEPTH, PEER_EXPERTS, D), PEER_HEADS ** -0.5)
    inp['ev_w_in'] = nrm((N_EVEN, D, EVEN_IN), D ** -0.5)
    inp['ev_q_norm'] = gain((N_EVEN, MLA_Q_RANK))
    inp['ev_kv_norm'] = gain((N_EVEN, MLA_KV_RANK))
    inp['ev_w_uq'] = nrm((N_EVEN, MLA_Q_RANK, MLA_HEADS * MLA_QK), MLA_Q_RANK ** -0.5)
    inp['ev_w_ukv'] = nrm((N_EVEN, MLA_KV_RANK, MLA_HEADS * (MLA_NOPE + MLA_V)), MLA_KV_RANK ** -0.5)
    inp['ev_q_gain'] = gain((N_EVEN, MLA_QK))
    inp['ev_k_gain'] = gain((N_EVEN, MLA_QK))
    inp['ev_conv_w'] = nrm((N_EVEN, SSD_CONV, SSD_CONV_DIM), SSD_CONV ** -0.5)
    inp['ev_conv_b'] = nrm((N_EVEN, SSD_CONV_DIM), 0.02)
    u = jax.random.uniform(next(ks), (N_EVEN, SSD_HEADS), jnp.float32)
    dt0 = jnp.exp(u * (math.log(0.1) - math.log(0.001)) + math.log(0.001))
    inp['ev_dt_bias'] = dt0 + jnp.log(-jnp.expm1(-dt0))
    inp['ev_a_log'] = jnp.log(jax.random.uniform(next(ks), (N_EVEN, SSD_HEADS), jnp.float32, 1.0, 16.0))
    inp['ev_d_skip'] = 1.0 + nrm((N_EVEN, SSD_HEADS), 0.1)
    inp['ev_ssd_norm'] = gain((N_EVEN, SSD_INNER))
    inp['ev_w_out'] = nrm((N_EVEN, EVEN_MIX, D), EVEN_MIX ** -0.5)
    inp['od_w_qkv'] = nrm((N_ODD, D, 2 * DIFF_HEADS * DIFF_HEAD_DIM * 2 + DIFF_MIX - DIFF_MIX), D ** -0.5) if False else nrm((N_ODD, D, 3 * D), D ** -0.5)
    inp['od_q_gain'] = gain((N_ODD, DIFF_HEAD_DIM))
    inp['od_k_gain'] = gain((N_ODD, DIFF_HEAD_DIM))
    inp['od_lam_q1'] = nrm((N_ODD, DIFF_HEAD_DIM), 0.1)
    inp['od_lam_k1'] = nrm((N_ODD, DIFF_HEAD_DIM), 0.1)
    inp['od_lam_q2'] = nrm((N_ODD, DIFF_HEAD_DIM), 0.1)
    inp['od_lam_k2'] = nrm((N_ODD, DIFF_HEAD_DIM), 0.1)
    inp['od_subln'] = gain((N_ODD, DIFF_V))
    inp['od_w_o'] = nrm((N_ODD, DIFF_MIX, D), DIFF_MIX ** -0.5)
    return inp


def reference(x, mem, positions,
              norm_mix, norm_mem, norm_mem_kv, norm_ffn,
              mem_w_q, mem_w_kv, mem_q_gain, mem_k_gain, mem_w_o,
              peer_w_q, peer_sub_keys, peer_u, peer_v,
              ev_w_in, ev_q_norm, ev_kv_norm, ev_w_uq, ev_w_ukv, ev_q_gain, ev_k_gain,
              ev_conv_w, ev_conv_b, ev_dt_bias, ev_a_log, ev_d_skip, ev_ssd_norm, ev_w_out,
              od_w_qkv, od_q_gain, od_k_gain, od_lam_q1, od_lam_k1, od_lam_q2, od_lam_k2,
              od_subln, od_w_o):
    for layer in range(DEPTH):
        i = layer // 2
        xn = rms_norm(x, norm_mix[layer])
        if layer % 2 == 0:
            h = even_mixer(xn, positions, ev_w_in[i], ev_q_norm[i], ev_kv_norm[i], ev_w_uq[i],
                           ev_w_ukv[i], ev_q_gain[i], ev_k_gain[i], ev_conv_w[i], ev_conv_b[i],
                           ev_dt_bias[i], ev_a_log[i], ev_d_skip[i], ev_ssd_norm[i], ev_w_out[i])
        else:
            h = odd_mixer(xn, positions, layer, od_w_qkv[i], od_q_gain[i], od_k_gain[i],
                          od_lam_q1[i], od_lam_k1[i], od_lam_q2[i], od_lam_k2[i],
                          od_subln[i], od_w_o[i])
        x = x + h
        x = x + memory_attention(rms_norm(x, norm_mem[layer]), rms_norm(mem, norm_mem_kv[layer]),
                                 mem_w_q[layer], mem_w_kv[layer], mem_q_gain[layer],
                                 mem_k_gain[layer], mem_w_o[layer])
        x = x + peer(rms_norm(x, norm_ffn[layer]), peer_w_q[layer], peer_sub_keys[layer],
                     peer_u[layer], peer_v[layer])
    return x
```

```python
import functools
import math

import jax
import jax.numpy as jnp
import numpy as np
from jax import lax
from jax.experimental import pallas as pl
from jax.experimental.pallas import tpu as pltpu

F32 = jnp.float32
BF16 = jnp.bfloat16

EPS = 1e-6
ROPE_THETA = 10000.0
LANES = 128

MLA_HEADS = 8
MLA_Q_RANK = 512
MLA_KV_RANK = 512
MLA_NOPE = 128
MLA_ROPE = 64
MLA_V = 128
MLA_QK = MLA_NOPE + MLA_ROPE
MLA_QK_PAD = 256

SSD_HEADS = 16
SSD_HEAD_DIM = 64
SSD_INNER = SSD_HEADS * SSD_HEAD_DIM
SSD_GROUPS = 4
SSD_STATE = 128
SSD_CONV = 4
SSD_CHUNK = 128
SSD_CONV_DIM = SSD_INNER + 2 * SSD_GROUPS * SSD_STATE
SSD_HPG = SSD_HEADS // SSD_GROUPS

DIFF_HEADS = 8
DIFF_HEAD_DIM = 128
DIFF_V = 256

MEM_HEADS = 4
MEM_HEAD_DIM = 128
MEM_WIDTH = MEM_HEADS * MEM_HEAD_DIM

PEER_HEADS = 8
PEER_KEYS = 128
PEER_TOPK = 16
PEER_HALF = 128

IN_XBC = 0
IN_Z = IN_XBC + SSD_CONV_DIM
IN_CQ = IN_Z + SSD_INNER
IN_CKV = IN_CQ + MLA_Q_RANK
IN_KR = IN_CKV + MLA_KV_RANK
IN_DT = IN_KR + LANES
IN_TOTAL = IN_DT + LANES

NEG_BIG = -1e30


def _cparams(sem, vmem_mb):
    return pltpu.CompilerParams(dimension_semantics=sem, vmem_limit_bytes=vmem_mb * 1024 * 1024)


def _rms(x, g, n=None):
    ss = jnp.sum(x * x, axis=-1, keepdims=True)
    n = x.shape[-1] if n is None else n
    return x * lax.rsqrt(ss * (1.0 / n) + EPS) * g


def _dot_nt(a, b):
    return lax.dot_general(a, b, (((1,), (1,)), ((), ())), preferred_element_type=F32)


def _norm_mm_kernel(x_ref, g_ref, w_ref, o_ref, xn_ref):
    @pl.when(pl.program_id(1) == 0)
    def _():
        xn_ref[...] = _rms(x_ref[...], g_ref[...]).astype(BF16)

    o_ref[...] = jnp.dot(xn_ref[...], w_ref[...], preferred_element_type=F32).astype(o_ref.dtype)


def norm_matmul(x, g, w, tm, tn, out_dtype=F32):
    t, d = x.shape
    n = w.shape[1]
    return pl.pallas_call(
        _norm_mm_kernel,
        grid=(t // tm, n // tn),
        in_specs=[pl.BlockSpec((tm, d), lambda i, j: (i, 0)),
                  pl.BlockSpec((1, d), lambda i, j: (0, 0)),
                  pl.BlockSpec((d, tn), lambda i, j: (0, j))],
        out_specs=pl.BlockSpec((tm, tn), lambda i, j: (i, j)),
        out_shape=jax.ShapeDtypeStruct((t, n), out_dtype),
        scratch_shapes=[pltpu.VMEM((tm, d), BF16)],
        compiler_params=_cparams(("parallel", "arbitrary"), 56),
        name="norm_matmul",
    )(x, g.reshape(1, d), w)


def _mm_res_kernel(*refs, n_a):
    res_ref, o_ref = refs[2 * n_a], refs[2 * n_a + 1]
    acc = res_ref[...]
    for a_ref, w_ref in zip(refs[:n_a], refs[n_a:2 * n_a]):
        acc = acc + jnp.dot(a_ref[...], w_ref[...], preferred_element_type=F32)
    o_ref[...] = acc


def matmul_residual(a_list, w_list, res, tm, tn):
    t, n = res.shape
    n_a = len(a_list)
    in_specs = ([pl.BlockSpec((tm, a.shape[1]), lambda i, j: (i, 0)) for a in a_list]
                + [pl.BlockSpec((w.shape[0], tn), lambda i, j: (0, j)) for w in w_list]
                + [pl.BlockSpec((tm, tn), lambda i, j: (i, j))])
    return pl.pallas_call(
        functools.partial(_mm_res_kernel, n_a=n_a),
        grid=(t // tm, n // tn),
        in_specs=in_specs,
        out_specs=pl.BlockSpec((tm, tn), lambda i, j: (i, j)),
        out_shape=jax.ShapeDtypeStruct((t, n), F32),
        compiler_params=_cparams(("parallel", "parallel"), 56),
        name="matmul_residual",
    )(*a_list, *w_list, res)


def _rope_tab_kernel(pos_ref, inv_ref, sgn_ref, cos_ref, sin_ref):
    ang = pos_ref[...].astype(F32) * inv_ref[...]
    cos_ref[...] = jnp.cos(ang)
    sin_ref[...] = jnp.sin(ang) * sgn_ref[...]


def rope_tables(pos, half, stride):
    t = pos.shape[0]
    lane = np.arange(LANES)
    j = lane % stride
    inv = np.where(j < half, 1.0 / (ROPE_THETA ** (np.minimum(j, half - 1).astype(np.float32) / half)), 0.0)
    sgn = np.where(lane < stride, -1.0, 1.0)
    tm = min(t, 1024)
    return pl.pallas_call(
        _rope_tab_kernel,
        grid=(t // tm,),
        in_specs=[pl.BlockSpec((tm, 1), lambda i: (i, 0)),
                  pl.BlockSpec((1, LANES), lambda i: (0, 0)),
                  pl.BlockSpec((1, LANES), lambda i: (0, 0))],
        out_specs=[pl.BlockSpec((tm, LANES), lambda i: (i, 0))] * 2,
        out_shape=[jax.ShapeDtypeStruct((t, LANES), F32)] * 2,
        compiler_params=_cparams(("parallel",), 32),
        name="rope_tables",
    )(pos.reshape(t, 1), jnp.asarray(inv, F32).reshape(1, LANES), jnp.asarray(sgn, F32).reshape(1, LANES))


def _rope_seg(x, cos, sin):
    return x * cos + pltpu.roll(x, LANES // 2, 1) * sin


def _flash(q, k_ref, v_ref, koff, dk, n_full, tq, tk):
    dv = v_ref.shape[-1]

    def step(j, carry, masked):
        m, l, acc = carry
        start = pl.multiple_of(j * tk, tk)
        k = k_ref[pl.ds(start, tk), koff:koff + dk]
        v = v_ref[pl.ds(start, tk), :]
        s = _dot_nt(q, k)
        if masked:
            row = lax.broadcasted_iota(jnp.int32, (tq, tk), 0)
            col = lax.broadcasted_iota(jnp.int32, (tq, tk), 1)
            s = jnp.where(col <= row, s, NEG_BIG)
        m_new = jnp.maximum(m, jnp.max(s, axis=-1, keepdims=True))
        p = jnp.exp(s - m_new)
        alpha = jnp.exp(m - m_new)
        l = alpha * l + jnp.sum(p, axis=-1, keepdims=True)
        acc = alpha * acc + jnp.dot(p.astype(BF16), v, preferred_element_type=F32)
        return m_new, l, acc

    init = (jnp.full((tq, 1), NEG_BIG, F32), jnp.zeros((tq, 1), F32), jnp.zeros((tq, dv), F32))
    carry = lax.fori_loop(0, n_full, lambda j, c: step(j, c, False), init)
    _, l, acc = step(n_full, carry, True)
    return acc / l


def _mla_prep_kernel(cq_ref, ckv_ref, kr_ref, qn_ref, kvn_ref, wuq_ref, wukv_ref, qg_ref, kg_ref,
                     cos_ref, sin_ref, q_out, k_out, v_out):
    cq = _rms(cq_ref[...], qn_ref[...]).astype(BF16)
    qf = jnp.dot(cq, wuq_ref[...], preferred_element_type=F32)
    ckv = _rms(ckv_ref[...], kvn_ref[...]).astype(BF16)
    kvf = jnp.dot(ckv, wukv_ref[...], preferred_element_type=F32)
    kr = kr_ref[...]
    kr_ss = jnp.sum(kr * kr, axis=-1, keepdims=True)
    cos, sin = cos_ref[...], sin_ref[...]
    qg, kg = qg_ref[...], kg_ref[...]
    scale = MLA_QK ** -0.5
    for h in range(MLA_HEADS):
        lo = h * MLA_QK_PAD
        qh = _rms(qf[:, lo:lo + MLA_QK_PAD], qg, MLA_QK) * scale
        q_out[:, lo:lo + LANES] = qh[:, :LANES].astype(BF16)
        q_out[:, lo + LANES:lo + 2 * LANES] = _rope_seg(qh[:, LANES:], cos, sin).astype(BF16)
        kn = kvf[:, lo:lo + MLA_NOPE]
        r = lax.rsqrt((jnp.sum(kn * kn, axis=-1, keepdims=True) + kr_ss) * (1.0 / MLA_QK) + EPS)
        k_out[:, lo:lo + LANES] = (kn * r * kg[:, :LANES]).astype(BF16)
        k_out[:, lo + LANES:lo + 2 * LANES] = _rope_seg(kr * r * kg[:, LANES:], cos, sin).astype(BF16)
        v_out[:, h * MLA_V:(h + 1) * MLA_V] = kvf[:, lo + MLA_NOPE:lo + MLA_QK_PAD].astype(BF16)


def _pad_rope_cols(w):
    half = MLA_ROPE // 2
    z = jnp.zeros(w.shape[:-1] + (half,), w.dtype)
    return jnp.concatenate([w[..., :half], z, w[..., half:], z], axis=-1)


def mla_prep(proj, q_norm, kv_norm, w_uq, w_ukv, q_gain, k_gain, cos, sin, tm):
    t = proj.shape[0]
    hq = MLA_HEADS * MLA_QK_PAD
    wq = w_uq.reshape(MLA_Q_RANK, MLA_HEADS, MLA_QK)
    wq = jnp.concatenate([wq[..., :MLA_NOPE], _pad_rope_cols(wq[..., MLA_NOPE:])], axis=-1)
    wq = wq.reshape(MLA_Q_RANK, hq).astype(BF16)
    qg = jnp.concatenate([q_gain[:MLA_NOPE], _pad_rope_cols(q_gain[MLA_NOPE:])]).reshape(1, MLA_QK_PAD)
    kg = jnp.concatenate([k_gain[:MLA_NOPE], _pad_rope_cols(k_gain[MLA_NOPE:])]).reshape(1, MLA_QK_PAD)
    const = lambda shape: pl.BlockSpec(shape, lambda i: (0, 0))
    return pl.pallas_call(
        _mla_prep_kernel,
        grid=(t // tm,),
        in_specs=[pl.BlockSpec((tm, MLA_Q_RANK), lambda i: (i, IN_CQ // MLA_Q_RANK)),
                  pl.BlockSpec((tm, MLA_KV_RANK), lambda i: (i, IN_CKV // MLA_KV_RANK)),
                  pl.BlockSpec((tm, LANES), lambda i: (i, IN_KR // LANES)),
                  const((1, MLA_Q_RANK)), const((1, MLA_KV_RANK)),
                  const((MLA_Q_RANK, hq)), const((MLA_KV_RANK, hq)),
                  const((1, MLA_QK_PAD)), const((1, MLA_QK_PAD)),
                  pl.BlockSpec((tm, LANES), lambda i: (i, 0)),
                  pl.BlockSpec((tm, LANES), lambda i: (i, 0))],
        out_specs=[pl.BlockSpec((tm, hq), lambda i: (i, 0)),
                   pl.BlockSpec((tm, hq), lambda i: (i, 0)),
                   pl.BlockSpec((tm, MLA_HEADS * MLA_V), lambda i: (i, 0))],
        out_shape=[jax.ShapeDtypeStruct((t, hq), BF16), jax.ShapeDtypeStruct((t, hq), BF16),
                   jax.ShapeDtypeStruct((t, MLA_HEADS * MLA_V), BF16)],
        compiler_params=_cparams(("parallel",), 56),
        name="mla_prep",
    )(proj, proj, proj, q_norm.reshape(1, -1), kv_norm.reshape(1, -1), wq, w_ukv.astype(BF16), qg, kg, cos, sin)


def _mla_attn_kernel(q_ref, k_ref, v_ref, o_ref, *, tq):
    o = _flash(q_ref[...], k_ref, v_ref, 0, MLA_QK_PAD, pl.program_id(2), tq, tq)
    o_ref[...] = o.astype(o_ref.dtype)


def mla_attention(q, k, v, bsz, s, tq):
    hq = MLA_HEADS * MLA_QK_PAD
    q, k = q.reshape(bsz, s, hq), k.reshape(bsz, s, hq)
    v = v.reshape(bsz, s, MLA_HEADS * MLA_V)
    out = pl.pallas_call(
        functools.partial(_mla_attn_kernel, tq=tq),
        grid=(bsz, MLA_HEADS, s // tq),
        in_specs=[pl.BlockSpec((None, tq, MLA_QK_PAD), lambda b, h, i: (b, i, h)),
                  pl.BlockSpec((None, s, MLA_QK_PAD), lambda b, h, i: (b, 0, h)),
                  pl.BlockSpec((None, s, MLA_V), lambda b, h, i: (b, 0, h))],
        out_specs=pl.BlockSpec((None, tq, MLA_V), lambda b, h, i: (b, i, h)),
        out_shape=jax.ShapeDtypeStruct((bsz, s, MLA_HEADS * MLA_V), BF16),
        compiler_params=_cparams(("parallel", "parallel", "arbitrary"), 48),
        name="mla_attention",
    )(q, k, v)
    return out.reshape(bsz * s, MLA_HEADS * MLA_V)


def _silu(x):
    return x * (1.0 / (1.0 + jnp.exp(-x)))


def _softplus(x):
    return jnp.maximum(x, 0.0) + jnp.log(1.0 + jnp.exp(-jnp.abs(x)))


def _ssd_kernel(xbc_ref, z_ref, dt_ref, cw_ref, cb_ref, dtb_ref, alog_ref, dsk_ref, nrm_ref, y_ref,
                xc_ref, st_ref):
    c = pl.program_id(1)
    L, P, N, G, HPG = SSD_CHUNK, SSD_HEAD_DIM, SSD_STATE, SSD_GROUPS, SSD_HPG
    TAIL = 8

    @pl.when(c == 0)
    def _():
        xc_ref[0:TAIL, :] = jnp.zeros((TAIL, SSD_CONV_DIM), F32)
        st_ref[...] = jnp.zeros(st_ref.shape, F32)

    xc_ref[TAIL:TAIL + L, :] = xbc_ref[...]
    conv = cb_ref[...] + jnp.zeros((L, SSD_CONV_DIM), F32)
    for k in range(SSD_CONV):
        off = TAIL - (SSD_CONV - 1) + k
        conv = conv + xc_ref[off:off + L, :] * cw_ref[k:k + 1, :]
    xc_ref[0:TAIL, :] = xc_ref[L:L + TAIL, :]
    xbc = _silu(conv)

    dt = _softplus(dt_ref[...] + dtb_ref[...])
    adt = dt * (-jnp.exp(alog_ref[...]))
    li = lax.broadcasted_iota(jnp.int32, (L, L), 0)
    si = lax.broadcasted_iota(jnp.int32, (L, L), 1)
    tri = (si <= li).astype(F32)
    a_cs = jnp.dot(tri, adt, preferred_element_type=F32, precision=lax.Precision.HIGHEST)
    a_cs_t = a_cs.T
    a_last = a_cs[L - 1:L, :]
    dec_state = jnp.exp(a_last - a_cs)
    dec_out = jnp.exp(a_cs)
    dec_chunk = jnp.exp(a_last)
    causal = si <= li

    zs = _silu(z_ref[...])
    for g in range(G):
        bg = xbc[:, SSD_INNER + g * N:SSD_INNER + (g + 1) * N]
        cg = xbc[:, SSD_INNER + G * N + g * N:SSD_INNER + G * N + (g + 1) * N]
        bg16, cg16 = bg.astype(BF16), cg.astype(BF16)
        cb = _dot_nt(cg16, bg16)
        bgt16 = bg.T.astype(BF16)
        xs_g = xbc[:, g * HPG * P:(g + 1) * HPG * P]
        xdd, ydiag = [], []
        for hh in range(HPG):
            h = g * HPG + hh
            xs = xs_g[:, hh * P:(hh + 1) * P]
            xd = xs * dt[:, h:h + 1]
            seg = a_cs[:, h:h + 1] - a_cs_t[h:h + 1, :]
            lm = cb * jnp.exp(jnp.where(causal, seg, NEG_BIG))
            ydiag.append(jnp.dot(lm.astype(BF16), xd.astype(BF16), preferred_element_type=F32))
            xdd.append((xd * dec_state[:, h:h + 1]).astype(BF16))
        prev = st_ref[g]
        yoff = jnp.dot(cg16, prev.astype(BF16), preferred_element_type=F32)
        new = jnp.dot(bgt16, jnp.concatenate(xdd, axis=1), preferred_element_type=F32)
        ys, decs = [], []
        for hh in range(HPG):
            h = g * HPG + hh
            xs = xs_g[:, hh * P:(hh + 1) * P]
            y = ydiag[hh] + yoff[:, hh * P:(hh + 1) * P] * dec_out[:, h:h + 1] + dsk_ref[:, h:h + 1] * xs
            ys.append(y)
            decs.append(jnp.broadcast_to(dec_chunk[:, h:h + 1], (N, P)))
        st_ref[g] = prev * jnp.concatenate(decs, axis=1) + new
        yg = jnp.concatenate(ys, axis=1) * zs[:, g * HPG * P:(g + 1) * HPG * P]
        y_ref[:, g * HPG * P:(g + 1) * HPG * P] = _rms(yg, nrm_ref[:, g * HPG * P:(g + 1) * HPG * P]).astype(y_ref.dtype)


def _pad_lanes(v):
    return jnp.pad(v.astype(F32), (0, LANES - v.shape[0])).reshape(1, LANES)


def ssd_mixer(proj, conv_w, conv_b, dt_bias, a_log, d_skip, ssd_norm, bsz, s):
    nc = s // SSD_CHUNK
    const = lambda shape: pl.BlockSpec(shape, lambda b, c: (0, 0))
    return pl.pallas_call(
        _ssd_kernel,
        grid=(bsz, nc),
        in_specs=[pl.BlockSpec((SSD_CHUNK, SSD_CONV_DIM), lambda b, c: (b * nc + c, IN_XBC // SSD_CONV_DIM)),
                  pl.BlockSpec((SSD_CHUNK, SSD_INNER), lambda b, c: (b * nc + c, IN_Z // SSD_INNER)),
                  pl.BlockSpec((SSD_CHUNK, LANES), lambda b, c: (b * nc + c, IN_DT // LANES)),
                  const((SSD_CONV, SSD_CONV_DIM)), const((1, SSD_CONV_DIM)),
                  const((1, LANES)), const((1, LANES)), const((1, LANES)), const((1, SSD_INNER))],
        out_specs=pl.BlockSpec((SSD_CHUNK, SSD_INNER), lambda b, c: (b * nc + c, 0)),
        out_shape=jax.ShapeDtypeStruct((bsz * s, SSD_INNER), BF16),
        scratch_shapes=[pltpu.VMEM((SSD_CHUNK + 8, SSD_CONV_DIM), F32),
                        pltpu.VMEM((SSD_GROUPS, SSD_STATE, SSD_HPG * SSD_HEAD_DIM), F32)],
        compiler_params=_cparams(("parallel", "arbitrary"), 48),
        name="ssd_mixer",
    )(proj, proj, proj, conv_w.astype(F32), conv_b.reshape(1, -1).astype(F32),
      _pad_lanes(dt_bias), _pad_lanes(a_log), _pad_lanes(d_skip), ssd_norm.reshape(1, -1).astype(F32))


def _diff_prep_kernel(q_ref, k_ref, v_ref, qg_ref, kg_ref, cos_ref, sin_ref, q_out, k_out, v_out):
    cos, sin = cos_ref[...], sin_ref[...]
    scale = DIFF_HEAD_DIM ** -0.5
    for m in range(2 * DIFF_HEADS):
        lo = m * DIFF_HEAD_DIM
        q = _rms(q_ref[:, lo:lo + DIFF_HEAD_DIM], qg_ref[...]) * scale
        q_out[:, lo:lo + DIFF_HEAD_DIM] = _rope_seg(q, cos, sin).astype(BF16)
        k = _rms(k_ref[:, lo:lo + DIFF_HEAD_DIM], kg_ref[...])
        k_out[:, lo:lo + DIFF_HEAD_DIM] = _rope_seg(k, cos, sin).astype(BF16)
    v_out[...] = v_ref[...].astype(BF16)


def diff_prep(qkv, q_gain, k_gain, cos, sin, tm):
    t = qkv.shape[0]
    d = 2 * DIFF_HEADS * DIFF_HEAD_DIM
    const = lambda shape: pl.BlockSpec(shape, lambda i: (0, 0))
    return pl.pallas_call(
        _diff_prep_kernel,
        grid=(t // tm,),
        in_specs=[pl.BlockSpec((tm, d), lambda i: (i, 0)),
                  pl.BlockSpec((tm, d), lambda i: (i, 1)),
                  pl.BlockSpec((tm, d), lambda i: (i, 2)),
                  const((1, DIFF_HEAD_DIM)), const((1, DIFF_HEAD_DIM)),
                  pl.BlockSpec((tm, LANES), lambda i: (i, 0)),
                  pl.BlockSpec((tm, LANES), lambda i: (i, 0))],
        out_specs=[pl.BlockSpec((tm, d), lambda i: (i, 0))] * 3,
        out_shape=[jax.ShapeDtypeStruct((t, d), BF16)] * 3,
        compiler_params=_cparams(("parallel",), 56),
        name="diff_prep",
    )(qkv, qkv, qkv, q_gain.reshape(1, -1), k_gain.reshape(1, -1), cos, sin)


def _diff_attn_kernel(q_ref, k_ref, v_ref, lq1_ref, lk1_ref, lq2_ref, lk2_ref, sub_ref, o_ref, *, tq, lam_init):
    i = pl.program_id(2)
    q = q_ref[...]
    o0 = _flash(q[:, :DIFF_HEAD_DIM], k_ref, v_ref, 0, DIFF_HEAD_DIM, i, tq, tq)
    o1 = _flash(q[:, DIFF_HEAD_DIM:], k_ref, v_ref, DIFF_HEAD_DIM, DIFF_HEAD_DIM, i, tq, tq)
    lam = (jnp.exp(jnp.sum(lq1_ref[...] * lk1_ref[...], axis=-1, keepdims=True))
           - jnp.exp(jnp.sum(lq2_ref[...] * lk2_ref[...], axis=-1, keepdims=True)) + lam_init)
    attn = o0 - lam * o1
    o_ref[...] = (_rms(attn, sub_ref[...]) * (1.0 - lam_init)).astype(o_ref.dtype)


def diff_attention(q, k, v, lam_q1, lam_k1, lam_q2, lam_k2, subln, lam_init, bsz, s, tq):
    d = 2 * DIFF_HEADS * DIFF_HEAD_DIM
    q, k, v = q.reshape(bsz, s, d), k.reshape(bsz, s, d), v.reshape(bsz, s, d)
    const = lambda shape: pl.BlockSpec(shape, lambda b, h, i: (0, 0))
    out = pl.pallas_call(
        functools.partial(_diff_attn_kernel, tq=tq, lam_init=lam_init),
        grid=(bsz, DIFF_HEADS, s // tq),
        in_specs=[pl.BlockSpec((None, tq, DIFF_V), lambda b, h, i: (b, i, h)),
                  pl.BlockSpec((None, s, DIFF_V), lambda b, h, i: (b, 0, h)),
                  pl.BlockSpec((None, s, DIFF_V), lambda b, h, i: (b, 0, h)),
                  const((1, DIFF_HEAD_DIM)), const((1, DIFF_HEAD_DIM)),
                  const((1, DIFF_HEAD_DIM)), const((1, DIFF_HEAD_DIM)), const((1, DIFF_V))],
        out_specs=pl.BlockSpec((None, tq, DIFF_V), lambda b, h, i: (b, i, h)),
        out_shape=jax.ShapeDtypeStruct((bsz, s, d), BF16),
        compiler_params=_cparams(("parallel", "parallel", "arbitrary"), 48),
        name="diff_attention",
    )(q, k, v, lam_q1.reshape(1, -1), lam_k1.reshape(1, -1), lam_q2.reshape(1, -1), lam_k2.reshape(1, -1),
      subln.reshape(1, -1))
    return out.reshape(bsz * s, d)


def _mem_attn_kernel(x_ref, g_ref, wq_ref, kv_ref, qg_ref, kg_ref, wo_ref, o_ref):
    x = x_ref[...]
    xn = _rms(x, g_ref[...]).astype(BF16)
    q = jnp.dot(xn, wq_ref[...], preferred_element_type=F32)
    kv = kv_ref[...]
    outs = []
    for h in range(MEM_HEADS):
        lo = h * MEM_HEAD_DIM
        qh = (_rms(q[:, lo:lo + MEM_HEAD_DIM], qg_ref[...]) * (MEM_HEAD_DIM ** -0.5)).astype(BF16)
        kh = _rms(kv[:, lo:lo + MEM_HEAD_DIM], kg_ref[...]).astype(BF16)
        vh = kv[:, MEM_WIDTH + lo:MEM_WIDTH + lo + MEM_HEAD_DIM].astype(BF16)
        s = _dot_nt(qh, kh)
        p = jnp.exp(s - jnp.max(s, axis=-1, keepdims=True))
        o = jnp.dot(p.astype(BF16), vh, preferred_element_type=F32)
        outs.append((o / jnp.sum(p, axis=-1, keepdims=True)).astype(BF16))
    o_ref[...] = x + jnp.dot(jnp.concatenate(outs, axis=1), wo_ref[...], preferred_element_type=F32)


def memory_attention(x, mem_kv, norm_g, w_q, q_gain, k_gain, w_o, bsz, s, tm):
    t, d = x.shape
    m = mem_kv.shape[0] // bsz
    nb = s // tm
    const = lambda shape: pl.BlockSpec(shape, lambda i: (0, 0))
    return pl.pallas_call(
        _mem_attn_kernel,
        grid=(t // tm,),
        in_specs=[pl.BlockSpec((tm, d), lambda i: (i, 0)),
                  const((1, d)), const((d, MEM_WIDTH)),
                  pl.BlockSpec((m, 2 * MEM_WIDTH), lambda i: (i // nb, 0)),
                  const((1, MEM_HEAD_DIM)), const((1, MEM_HEAD_DIM)), const((MEM_WIDTH, d))],
        out_specs=pl.BlockSpec((tm, d), lambda i: (i, 0)),
        out_shape=jax.ShapeDtypeStruct((t, d), F32),
        compiler_params=_cparams(("parallel",), 56),
        name="memory_attention",
    )(x, norm_g.reshape(1, d), w_q.astype(BF16), mem_kv, q_gain.reshape(1, -1), k_gain.reshape(1, -1),
      w_o.astype(BF16))


def _top_values(x, k):
    r = x.shape[0]
    rows = lax.broadcasted_iota(jnp.int32, x.shape, 0)
    out = []
    for _ in range(k):
        m = jnp.max(x, axis=0, keepdims=True)
        first = jnp.min(jnp.where(x == m, rows, r), axis=0, keepdims=True)
        x = jnp.where(rows == first, -jnp.inf, x)
        out.append(m)
    return out


_PAIRS = [(i, j) for i in range(PEER_TOPK) for j in range(PEER_TOPK) if (i + 1) * (j + 1) <= PEER_TOPK]
_PAIR_ROWS = -(-len(_PAIRS) // 8) * 8

ST_TAU, ST_A1, ST_B1, ST_ZINV = 0, PEER_HEADS, 2 * PEER_HEADS, 3 * PEER_HEADS


def _peer_route_kernel(x_ref, g_ref, wq_ref, sk_ref, xn_out, sc_out, st_out):
    tm = x_ref.shape[0]
    xn = _rms(x_ref[...], g_ref[...]).astype(BF16)
    xn_out[...] = xn
    q = jnp.dot(xn, wq_ref[...], preferred_element_type=F32).astype(BF16)
    for g in range(2 * PEER_HEADS):
        sc_out[g * PEER_KEYS:(g + 1) * PEER_KEYS, :] = _dot_nt(sk_ref[g], q[:, g * PEER_HALF:(g + 1) * PEER_HALF])
    for h in range(PEER_HEADS):
        lo = 2 * h * PEER_KEYS
        a = _top_values(sc_out[lo:lo + PEER_KEYS, :], PEER_TOPK)
        b = _top_values(sc_out[lo + PEER_KEYS:lo + 2 * PEER_KEYS, :], PEER_TOPK)
        cands = [a[i] + b[j] for i, j in _PAIRS]
        cands += [jnp.full((1, tm), -jnp.inf, F32)] * (_PAIR_ROWS - len(_PAIRS))
        cand = jnp.concatenate(cands, axis=0)
        tau = _top_values(cand, PEER_TOPK)[-1]
        mx = a[0] + b[0]
        z = jnp.sum(jnp.where(cand >= tau, jnp.exp(cand - mx), 0.0), axis=0, keepdims=True)
        st_out[ST_TAU + h:ST_TAU + h + 1, :] = tau
        st_out[ST_A1 + h:ST_A1 + h + 1, :] = a[0]
        st_out[ST_B1 + h:ST_B1 + h + 1, :] = b[0]
        st_out[ST_ZINV + h:ST_ZINV + h + 1, :] = 1.0 / z


def peer_route(x, norm_g, w_q, sub_keys, tm):
    t, d = x.shape
    nq = 2 * PEER_HEADS * PEER_HALF
    sk = sub_keys.reshape(2 * PEER_HEADS, PEER_KEYS, PEER_HALF).astype(BF16)
    return pl.pallas_call(
        _peer_route_kernel,
        grid=(t // tm,),
        in_specs=[pl.BlockSpec((tm, d), lambda i: (i, 0)),
                  pl.BlockSpec((1, d), lambda i: (0, 0)),
                  pl.BlockSpec((d, nq), lambda i: (0, 0)),
                  pl.BlockSpec((2 * PEER_HEADS, PEER_KEYS, PEER_HALF), lambda i: (0, 0, 0))],
        out_specs=[pl.BlockSpec((tm, d), lambda i: (i, 0)),
                   pl.BlockSpec((2 * PEER_HEADS * PEER_KEYS, tm), lambda i: (0, i)),
                   pl.BlockSpec((4 * PEER_HEADS, tm), lambda i: (0, i))],
        out_shape=[jax.ShapeDtypeStruct((t, d), BF16),
                   jax.ShapeDtypeStruct((2 * PEER_HEADS * PEER_KEYS, t), F32),
                   jax.ShapeDtypeStruct((4 * PEER_HEADS, t), F32)],
        compiler_params=_cparams(("parallel",), 56),
        name="peer_route",
    )(x, norm_g.reshape(1, d), w_q.astype(BF16), sk)


def _gelu(x):
    return 0.5 * x * (1.0 + lax.erf(x * (2.0 ** -0.5)))


def _peer_dense_kernel(xn_ref, sc_ref, st_ref, u_ref, vt_ref, res_ref, o_ref, acc_ref, e0_ref, e1_ref, w_ref, *, nr):
    e = pl.program_id(1)
    K = PEER_KEYS

    @pl.when(e == 0)
    def _():
        acc_ref[...] = jnp.zeros(acc_ref.shape, F32)
        for h in range(PEER_HEADS):
            lo = 2 * h * K
            e0_ref[h] = jnp.exp(sc_ref[lo:lo + K, :] - st_ref[ST_A1 + h:ST_A1 + h + 1, :])
            e1_ref[h] = (jnp.exp(sc_ref[lo + K:lo + 2 * K, :] - st_ref[ST_B1 + h:ST_B1 + h + 1, :])
                         * st_ref[ST_ZINV + h:ST_ZINV + h + 1, :])

    xu = _dot_nt(u_ref[...], xn_ref[...])
    for j in range(nr):
        r = e * nr + j
        gate = jnp.zeros((K, xu.shape[1]), F32)
        for h in range(PEER_HEADS):
            lo = 2 * h * K
            s = sc_ref[lo + K:lo + 2 * K, :] + sc_ref[pl.ds(lo + r, 1), :]
            gate = gate + jnp.where(s >= st_ref[ST_TAU + h:ST_TAU + h + 1, :],
                                    e1_ref[h] * e0_ref[h, pl.ds(r, 1), :], 0.0)
        w_ref[j * K:(j + 1) * K, :] = (_gelu(xu[j * K:(j + 1) * K, :]) * gate).astype(BF16)
    acc_ref[...] += jnp.dot(vt_ref[...], w_ref[...], preferred_element_type=F32)

    @pl.when(e == pl.num_programs(1) - 1)
    def _():
        o_ref[...] = res_ref[...] + acc_ref[...].T


def peer_dense(xn, sc_t, stats, u_tab, v_tab, res, tm, eb):
    t, d = res.shape
    n_exp = u_tab.shape[0]
    u16 = u_tab.astype(BF16)
    vt16 = v_tab.astype(BF16).T
    nsc = sc_t.shape[0]
    return pl.pallas_call(
        functools.partial(_peer_dense_kernel, nr=eb // PEER_KEYS),
        grid=(t // tm, n_exp // eb),
        in_specs=[pl.BlockSpec((tm, d), lambda i, e: (i, 0)),
                  pl.BlockSpec((nsc, tm), lambda i, e: (0, i)),
                  pl.BlockSpec((4 * PEER_HEADS, tm), lambda i, e: (0, i)),
                  pl.BlockSpec((eb, d), lambda i, e: (e, 0)),
                  pl.BlockSpec((d, eb), lambda i, e: (0, e)),
                  pl.BlockSpec((tm, d), lambda i, e: (i, 0))],
        out_specs=pl.BlockSpec((tm, d), lambda i, e: (i, 0)),
        out_shape=jax.ShapeDtypeStruct((t, d), F32),
        scratch_shapes=[pltpu.VMEM((d, tm), F32),
                        pltpu.VMEM((PEER_HEADS, PEER_KEYS, tm), F32),
                        pltpu.VMEM((PEER_HEADS, PEER_KEYS, tm), F32),
                        pltpu.VMEM((eb, tm), BF16)],
        compiler_params=_cparams(("parallel", "arbitrary"), 60),
        name="peer_dense",
    )(xn, sc_t, stats, u16, vt16, res)


def _even_w_in(w_in):
    o = np.cumsum([0, MLA_Q_RANK, MLA_KV_RANK, MLA_ROPE, SSD_INNER, SSD_CONV_DIM, SSD_HEADS])
    c_q, c_kv, k_r, z, xbc, dt = (w_in[:, o[i]:o[i + 1]] for i in range(6))
    dt = jnp.pad(dt, ((0, 0), (0, LANES - SSD_HEADS)))
    return jnp.concatenate([xbc, z, c_q, c_kv, _pad_rope_cols(k_r), dt], axis=1).astype(BF16)


def _tile(n, pref):
    return pref if n % pref == 0 else n


def kernel(x, mem, positions, norm_mix, norm_mem, norm_mem_kv, norm_ffn, mem_w_q, mem_w_kv, mem_q_gain, mem_k_gain, mem_w_o, peer_w_q, peer_sub_keys, peer_u, peer_v, ev_w_in, ev_q_norm, ev_kv_norm, ev_w_uq, ev_w_ukv, ev_q_gain, ev_k_gain, ev_conv_w, ev_conv_b, ev_dt_bias, ev_a_log, ev_d_skip, ev_ssd_norm, ev_w_out, od_w_qkv, od_q_gain, od_k_gain, od_lam_q1, od_lam_k1, od_lam_q2, od_lam_k2, od_subln, od_w_o):
    bsz, s, d = x.shape
    t = bsz * s
    depth = norm_mix.shape[0]
    tm = _tile(s, 512)
    tq = _tile(s, 512)
    xf = x.reshape(t, d)
    pos = positions.reshape(t)
    mem_f = mem.reshape(-1, d)
    cos_mla, sin_mla = rope_tables(pos, MLA_ROPE // 2, LANES // 2)
    cos_dif, sin_dif = rope_tables(pos, DIFF_HEAD_DIM // 2, LANES // 2)

    for layer in range(depth):
        i = layer // 2
        if layer % 2 == 0:
            proj = norm_matmul(xf, norm_mix[layer], _even_w_in(ev_w_in[i]), tm, IN_TOTAL // 2)
            q, k, v = mla_prep(proj, ev_q_norm[i], ev_kv_norm[i], ev_w_uq[i], ev_w_ukv[i],
                               ev_q_gain[i], ev_k_gain[i], cos_mla, sin_mla, tm)
            mla_out = mla_attention(q, k, v, bsz, s, tq)
            y = ssd_mixer(proj, ev_conv_w[i], ev_conv_b[i], ev_dt_bias[i], ev_a_log[i], ev_d_skip[i],
                          ev_ssd_norm[i], bsz, s)
            w_out = ev_w_out[i].astype(BF16)
            n_mla = MLA_HEADS * MLA_V
            xf = matmul_residual([mla_out, y], [w_out[:n_mla], w_out[n_mla:]], xf, tm, 1024)
        else:
            qkv = norm_matmul(xf, norm_mix[layer], od_w_qkv[i].astype(BF16), tm, 1024)
            q, k, v = diff_prep(qkv, od_q_gain[i], od_k_gain[i], cos_dif, sin_dif, tm)
            lam_init = 0.8 - 0.6 * math.exp(-0.3 * layer)
            attn = diff_attention(q, k, v, od_lam_q1[i], od_lam_k1[i], od_lam_q2[i], od_lam_k2[i],
                                  od_subln[i], lam_init, bsz, s, tq)
            xf = matmul_residual([attn], [od_w_o[i].astype(BF16)], xf, tm, 1024)
        mem_kv = norm_matmul(mem_f, norm_mem_kv[layer], mem_w_kv[layer].astype(BF16),
                             _tile(mem_f.shape[0], 512), 2 * MEM_WIDTH)
        xf = memory_attention(xf, mem_kv, norm_mem[layer], mem_w_q[layer], mem_q_gain[layer],
                              mem_k_gain[layer], mem_w_o[layer], bsz, s, tm)
        xn, sc_t, stats = peer_route(xf, norm_ffn[layer], peer_w_q[layer], peer_sub_keys[layer], tm)
        xf = peer_dense(xn, sc_t, stats, peer_u[layer], peer_v[layer], xf, tm, 512)
    return xf.reshape(bsz, s, d)
```

```python
import functools
import math

import jax
import jax.numpy as jnp
import numpy as np
from jax import lax
from jax.experimental import pallas as pl
from jax.experimental.pallas import tpu as pltpu

F32 = jnp.float32
BF16 = jnp.bfloat16

EPS = 1e-6
ROPE_THETA = 10000.0
LANES = 128

MLA_HEADS = 8
MLA_Q_RANK = 512
MLA_KV_RANK = 512
MLA_NOPE = 128
MLA_ROPE = 64
MLA_V = 128
MLA_QK = MLA_NOPE + MLA_ROPE
MLA_QK_PAD = 256

SSD_HEADS = 16
SSD_HEAD_DIM = 64
SSD_INNER = SSD_HEADS * SSD_HEAD_DIM
SSD_GROUPS = 4
SSD_STATE = 128
SSD_CONV = 4
SSD_CHUNK = 128
SSD_CONV_DIM = SSD_INNER + 2 * SSD_GROUPS * SSD_STATE
SSD_HPG = SSD_HEADS // SSD_GROUPS

DIFF_HEADS = 8
DIFF_HEAD_DIM = 128
DIFF_V = 256

MEM_HEADS = 4
MEM_HEAD_DIM = 128
MEM_WIDTH = MEM_HEADS * MEM_HEAD_DIM

PEER_HEADS = 8
PEER_KEYS = 128
PEER_TOPK = 16
PEER_HALF = 128

IN_XBC = 0
IN_Z = IN_XBC + SSD_CONV_DIM
IN_CQ = IN_Z + SSD_INNER
IN_CKV = IN_CQ + MLA_Q_RANK
IN_KR = IN_CKV + MLA_KV_RANK
IN_DT = IN_KR + LANES
IN_TOTAL = IN_DT + LANES

NEG_BIG = -1e30


def _cparams(sem, vmem_mb):
    return pltpu.CompilerParams(dimension_semantics=sem, vmem_limit_bytes=vmem_mb * 1024 * 1024)


def _rms(x, g, n=None):
    ss = jnp.sum(x * x, axis=-1, keepdims=True)
    n = x.shape[-1] if n is None else n
    return x * lax.rsqrt(ss * (1.0 / n) + EPS) * g


def _dot_nt(a, b):
    return lax.dot_general(a, b, (((1,), (1,)), ((), ())), preferred_element_type=F32)


def _norm_mm_kernel(x_ref, g_ref, w_ref, o_ref, xn_ref):
    @pl.when(pl.program_id(1) == 0)
    def _():
        xn_ref[...] = _rms(x_ref[...], g_ref[...]).astype(BF16)

    o_ref[...] = jnp.dot(xn_ref[...], w_ref[...], preferred_element_type=F32).astype(o_ref.dtype)


def norm_matmul(x, g, w, tm, tn, out_dtype=F32):
    t, d = x.shape
    n = w.shape[1]
    return pl.pallas_call(
        _norm_mm_kernel,
        grid=(t // tm, n // tn),
        in_specs=[pl.BlockSpec((tm, d), lambda i, j: (i, 0)),
                  pl.BlockSpec((1, d), lambda i, j: (0, 0)),
                  pl.BlockSpec((d, tn), lambda i, j: (0, j))],
        out_specs=pl.BlockSpec((tm, tn), lambda i, j: (i, j)),
        out_shape=jax.ShapeDtypeStruct((t, n), out_dtype),
        scratch_shapes=[pltpu.VMEM((tm, d), BF16)],
        compiler_params=_cparams(("parallel", "arbitrary"), 56),
        name="norm_matmul",
    )(x, g.reshape(1, d), w)


def _mm_res_kernel(*refs, n_a):
    res_ref, o_ref = refs[2 * n_a], refs[2 * n_a + 1]
    acc = res_ref[...]
    for a_ref, w_ref in zip(refs[:n_a], refs[n_a:2 * n_a]):
        acc = acc + jnp.dot(a_ref[...], w_ref[...], preferred_element_type=F32)
    o_ref[...] = acc


def matmul_residual(a_list, w_list, res, tm, tn):
    t, n = res.shape
    n_a = len(a_list)
    in_specs = ([pl.BlockSpec((tm, a.shape[1]), lambda i, j: (i, 0)) for a in a_list]
                + [pl.BlockSpec((w.shape[0], tn), lambda i, j: (0, j)) for w in w_list]
                + [pl.BlockSpec((tm, tn), lambda i, j: (i, j))])
    return pl.pallas_call(
        functools.partial(_mm_res_kernel, n_a=n_a),
        grid=(t // tm, n // tn),
        in_specs=in_specs,
        out_specs=pl.BlockSpec((tm, tn), lambda i, j: (i, j)),
        out_shape=jax.ShapeDtypeStruct((t, n), F32),
        compiler_params=_cparams(("parallel", "parallel"), 56),
        name="matmul_residual",
    )(*a_list, *w_list, res)


def _rope_tab_kernel(pos_ref, inv_ref, sgn_ref, cos_ref, sin_ref):
    ang = pos_ref[...].astype(F32) * inv_ref[...]
    cos_ref[...] = jnp.cos(ang)
    sin_ref[...] = jnp.sin(ang) * sgn_ref[...]


def rope_tables(pos, half, stride):
    t = pos.shape[0]
    lane = np.arange(LANES)
    j = lane % stride
    inv = np.where(j < half, 1.0 / (ROPE_THETA ** (np.minimum(j, half - 1).astype(np.float32) / half)), 0.0)
    sgn = np.where(lane < stride, -1.0, 1.0)
    tm = min(t, 1024)
    return pl.pallas_call(
        _rope_tab_kernel,
        grid=(t // tm,),
        in_specs=[pl.BlockSpec((tm, 1), lambda i: (i, 0)),
                  pl.BlockSpec((1, LANES), lambda i: (0, 0)),
                  pl.BlockSpec((1, LANES), lambda i: (0, 0))],
        out_specs=[pl.BlockSpec((tm, LANES), lambda i: (i, 0))] * 2,
        out_shape=[jax.ShapeDtypeStruct((t, LANES), F32)] * 2,
        compiler_params=_cparams(("parallel",), 32),
        name="rope_tables",
    )(pos.reshape(t, 1), jnp.asarray(inv, F32).reshape(1, LANES), jnp.asarray(sgn, F32).reshape(1, LANES))


def _rope_seg(x, cos, sin):
    return x * cos + pltpu.roll(x, LANES // 2, 1) * sin


def _rep(x, n):
    return x if n == 1 else jnp.concatenate([x] * n, axis=1)


def _flash(chains, n_full, tq, tk):
    for (_, _, _, _, _, _, _, m_ref, l_ref, acc_ref) in chains:
        m_ref[...] = jnp.full(m_ref.shape, NEG_BIG, F32)
        l_ref[...] = jnp.zeros(l_ref.shape, F32)
        acc_ref[...] = jnp.zeros(acc_ref.shape, F32)

    def step(j, masked):
        start = pl.multiple_of(j * tk, tk)
        if masked:
            row = lax.broadcasted_iota(jnp.int32, (tq, tk), 0)
            col = lax.broadcasted_iota(jnp.int32, (tq, tk), 1)
            keep = col <= row
        for (q, k_ref, koff, dk, v_ref, voff, dv, m_ref, l_ref, acc_ref) in chains:
            s = _dot_nt(q, k_ref[pl.ds(start, tk), koff:koff + dk])
            if masked:
                s = jnp.where(keep, s, NEG_BIG)
            m_prev = m_ref[...]
            m_new = jnp.maximum(m_prev, jnp.max(s, axis=-1, keepdims=True))
            p = jnp.exp(s - _rep(m_new, tk // LANES))
            alpha = jnp.exp(m_prev - m_new)
            l_ref[...] = alpha * l_ref[...] + jnp.sum(p, axis=-1, keepdims=True)
            acc_ref[...] = (_rep(alpha, dv // LANES) * acc_ref[...]
                            + jnp.dot(p.astype(BF16), v_ref[pl.ds(start, tk), voff:voff + dv],
                                      preferred_element_type=F32))
            m_ref[...] = m_new

    def body(j, carry):
        step(j, False)
        return carry

    lax.fori_loop(0, n_full, body, 0)
    step(n_full, True)
    return [acc_ref[...] / _rep(l_ref[...], dv // LANES) for (_, _, _, _, _, _, dv, _, l_ref, acc_ref) in chains]


def _mla_prep_kernel(cq_ref, ckv_ref, kr_ref, qn_ref, kvn_ref, wuq_ref, wukv_ref, qg_ref, kg_ref,
                     cos_ref, sin_ref, q_out, k_out, v_out):
    cq = _rms(cq_ref[...], qn_ref[...]).astype(BF16)
    qf = jnp.dot(cq, wuq_ref[...], preferred_element_type=F32)
    ckv = _rms(ckv_ref[...], kvn_ref[...]).astype(BF16)
    kvf = jnp.dot(ckv, wukv_ref[...], preferred_element_type=F32)
    kr = kr_ref[...]
    kr_ss = jnp.sum(kr * kr, axis=-1, keepdims=True)
    cos, sin = cos_ref[...], sin_ref[...]
    qg, kg = qg_ref[...], kg_ref[...]
    scale = MLA_QK ** -0.5
    for h in range(MLA_HEADS):
        lo = h * MLA_QK_PAD
        qh = _rms(qf[:, lo:lo + MLA_QK_PAD], qg, MLA_QK) * scale
        q_out[:, lo:lo + LANES] = qh[:, :LANES].astype(BF16)
        q_out[:, lo + LANES:lo + 2 * LANES] = _rope_seg(qh[:, LANES:], cos, sin).astype(BF16)
        kn = kvf[:, lo:lo + MLA_NOPE]
        r = lax.rsqrt((jnp.sum(kn * kn, axis=-1, keepdims=True) + kr_ss) * (1.0 / MLA_QK) + EPS)
        k_out[:, lo:lo + LANES] = (kn * r * kg[:, :LANES]).astype(BF16)
        k_out[:, lo + LANES:lo + 2 * LANES] = _rope_seg(kr * r * kg[:, LANES:], cos, sin).astype(BF16)
        v_out[:, h * MLA_V:(h + 1) * MLA_V] = kvf[:, lo + MLA_NOPE:lo + MLA_QK_PAD].astype(BF16)


def _pad_rope_cols(w):
    half = MLA_ROPE // 2
    z = jnp.zeros(w.shape[:-1] + (half,), w.dtype)
    return jnp.concatenate([w[..., :half], z, w[..., half:], z], axis=-1)


def mla_prep(proj, q_norm, kv_norm, w_uq, w_ukv, q_gain, k_gain, cos, sin, tm):
    t = proj.shape[0]
    hq = MLA_HEADS * MLA_QK_PAD
    wq = w_uq.reshape(MLA_Q_RANK, MLA_HEADS, MLA_QK)
    wq = jnp.concatenate([wq[..., :MLA_NOPE], _pad_rope_cols(wq[..., MLA_NOPE:])], axis=-1)
    wq = wq.reshape(MLA_Q_RANK, hq).astype(BF16)
    qg = jnp.concatenate([q_gain[:MLA_NOPE], _pad_rope_cols(q_gain[MLA_NOPE:])]).reshape(1, MLA_QK_PAD)
    kg = jnp.concatenate([k_gain[:MLA_NOPE], _pad_rope_cols(k_gain[MLA_NOPE:])]).reshape(1, MLA_QK_PAD)
    const = lambda shape: pl.BlockSpec(shape, lambda i: (0, 0))
    return pl.pallas_call(
        _mla_prep_kernel,
        grid=(t // tm,),
        in_specs=[pl.BlockSpec((tm, MLA_Q_RANK), lambda i: (i, IN_CQ // MLA_Q_RANK)),
                  pl.BlockSpec((tm, MLA_KV_RANK), lambda i: (i, IN_CKV // MLA_KV_RANK)),
                  pl.BlockSpec((tm, LANES), lambda i: (i, IN_KR // LANES)),
                  const((1, MLA_Q_RANK)), const((1, MLA_KV_RANK)),
                  const((MLA_Q_RANK, hq)), const((MLA_KV_RANK, hq)),
                  const((1, MLA_QK_PAD)), const((1, MLA_QK_PAD)),
                  pl.BlockSpec((tm, LANES), lambda i: (i, 0)),
                  pl.BlockSpec((tm, LANES), lambda i: (i, 0))],
        out_specs=[pl.BlockSpec((tm, hq), lambda i: (i, 0)),
                   pl.BlockSpec((tm, hq), lambda i: (i, 0)),
                   pl.BlockSpec((tm, MLA_HEADS * MLA_V), lambda i: (i, 0))],
        out_shape=[jax.ShapeDtypeStruct((t, hq), BF16), jax.ShapeDtypeStruct((t, hq), BF16),
                   jax.ShapeDtypeStruct((t, MLA_HEADS * MLA_V), BF16)],
        compiler_params=_cparams(("parallel",), 56),
        name="mla_prep",
    )(proj, proj, proj, q_norm.reshape(1, -1), kv_norm.reshape(1, -1), wq, w_ukv.astype(BF16), qg, kg, cos, sin)


ATTN_HPS = 2


def _mla_attn_kernel(q_ref, k_ref, v_ref, o_ref, m_ref, l_ref, acc_ref, *, tq):
    chains = [(q_ref[:, h * MLA_QK_PAD:(h + 1) * MLA_QK_PAD], k_ref, h * MLA_QK_PAD, MLA_QK_PAD,
               v_ref, h * MLA_V, MLA_V, m_ref.at[h], l_ref.at[h], acc_ref.at[h]) for h in range(ATTN_HPS)]
    outs = _flash(chains, pl.program_id(2), tq, tq)
    for h in range(ATTN_HPS):
        o_ref[:, h * MLA_V:(h + 1) * MLA_V] = outs[h].astype(o_ref.dtype)


def mla_attention(q, k, v, bsz, s, tq):
    hq = MLA_HEADS * MLA_QK_PAD
    q, k = q.reshape(bsz, s, hq), k.reshape(bsz, s, hq)
    v = v.reshape(bsz, s, MLA_HEADS * MLA_V)
    hps = ATTN_HPS
    out = pl.pallas_call(
        functools.partial(_mla_attn_kernel, tq=tq),
        grid=(bsz, MLA_HEADS // hps, s // tq),
        in_specs=[pl.BlockSpec((None, tq, hps * MLA_QK_PAD), lambda b, h, i: (b, i, h)),
                  pl.BlockSpec((None, s, hps * MLA_QK_PAD), lambda b, h, i: (b, 0, h)),
                  pl.BlockSpec((None, s, hps * MLA_V), lambda b, h, i: (b, 0, h))],
        out_specs=pl.BlockSpec((None, tq, hps * MLA_V), lambda b, h, i: (b, i, h)),
        out_shape=jax.ShapeDtypeStruct((bsz, s, MLA_HEADS * MLA_V), BF16),
        scratch_shapes=[pltpu.VMEM((hps, tq, LANES), F32), pltpu.VMEM((hps, tq, LANES), F32),
                        pltpu.VMEM((hps, tq, MLA_V), F32)],
        compiler_params=_cparams(("parallel", "parallel", "arbitrary"), 48),
        name="mla_attention",
    )(q, k, v)
    return out.reshape(bsz * s, MLA_HEADS * MLA_V)


def _silu(x):
    return x * (1.0 / (1.0 + jnp.exp(-x)))


def _softplus(x):
    return jnp.maximum(x, 0.0) + jnp.log(1.0 + jnp.exp(-jnp.abs(x)))


def _ssd_kernel(xbc_ref, z_ref, dt_ref, cw_ref, cb_ref, dtb_ref, alog_ref, dsk_ref, nrm_ref, y_ref,
                xc_ref, st_ref):
    c = pl.program_id(1)
    L, P, N, G, HPG = SSD_CHUNK, SSD_HEAD_DIM, SSD_STATE, SSD_GROUPS, SSD_HPG
    TAIL = 8

    @pl.when(c == 0)
    def _():
        xc_ref[0:TAIL, :] = jnp.zeros((TAIL, SSD_CONV_DIM), F32)
        st_ref[...] = jnp.zeros(st_ref.shape, F32)

    xc_ref[TAIL:TAIL + L, :] = xbc_ref[...]
    conv = cb_ref[...] + jnp.zeros((L, SSD_CONV_DIM), F32)
    for k in range(SSD_CONV):
        off = TAIL - (SSD_CONV - 1) + k
        conv = conv + xc_ref[off:off + L, :] * cw_ref[k:k + 1, :]
    xc_ref[0:TAIL, :] = xc_ref[L:L + TAIL, :]
    xbc = _silu(conv)

    dt = _softplus(dt_ref[...] + dtb_ref[...])
    adt = dt * (-jnp.exp(alog_ref[...]))
    li = lax.broadcasted_iota(jnp.int32, (L, L), 0)
    si = lax.broadcasted_iota(jnp.int32, (L, L), 1)
    tri = (si <= li).astype(F32)
    a_cs = jnp.dot(tri, adt, preferred_element_type=F32, precision=lax.Precision.HIGHEST)
    a_cs_t = a_cs.T
    a_last = a_cs[L - 1:L, :]
    dec_state = jnp.exp(a_last - a_cs)
    dec_out = jnp.exp(a_cs)
    dec_chunk = jnp.exp(a_last)
    causal = si <= li

    zs = _silu(z_ref[...])
    for g in range(G):
        bg = xbc[:, SSD_INNER + g * N:SSD_INNER + (g + 1) * N]
        cg = xbc[:, SSD_INNER + G * N + g * N:SSD_INNER + G * N + (g + 1) * N]
        bg16, cg16 = bg.astype(BF16), cg.astype(BF16)
        cb = _dot_nt(cg16, bg16)
        bgt16 = bg.T.astype(BF16)
        xs_g = xbc[:, g * HPG * P:(g + 1) * HPG * P]
        xdd, ydiag = [], []
        for hh in range(HPG):
            h = g * HPG + hh
            xs = xs_g[:, hh * P:(hh + 1) * P]
            xd = xs * dt[:, h:h + 1]
            seg = a_cs[:, h:h + 1] - a_cs_t[h:h + 1, :]
            lm = cb * jnp.exp(jnp.where(causal, seg, NEG_BIG))
            ydiag.append(jnp.dot(lm.astype(BF16), xd.astype(BF16), preferred_element_type=F32))
            xdd.append((xd * dec_state[:, h:h + 1]).astype(BF16))
        prev = st_ref[g]
        yoff = jnp.dot(cg16, prev.astype(BF16), preferred_element_type=F32)
        new = jnp.dot(bgt16, jnp.concatenate(xdd, axis=1), preferred_element_type=F32)
        ys, decs = [], []
        for hh in range(HPG):
            h = g * HPG + hh
            xs = xs_g[:, hh * P:(hh + 1) * P]
            y = ydiag[hh] + yoff[:, hh * P:(hh + 1) * P] * dec_out[:, h:h + 1] + dsk_ref[:, h:h + 1] * xs
            ys.append(y)
            decs.append(jnp.broadcast_to(dec_chunk[:, h:h + 1], (N, P)))
        st_ref[g] = prev * jnp.concatenate(decs, axis=1) + new
        yg = jnp.concatenate(ys, axis=1) * zs[:, g * HPG * P:(g + 1) * HPG * P]
        y_ref[:, g * HPG * P:(g + 1) * HPG * P] = _rms(yg, nrm_ref[:, g * HPG * P:(g + 1) * HPG * P]).astype(y_ref.dtype)


def _pad_lanes(v):
    return jnp.pad(v.astype(F32), (0, LANES - v.shape[0])).reshape(1, LANES)


def ssd_mixer(proj, conv_w, conv_b, dt_bias, a_log, d_skip, ssd_norm, bsz, s):
    nc = s // SSD_CHUNK
    const = lambda shape: pl.BlockSpec(shape, lambda b, c: (0, 0))
    return pl.pallas_call(
        _ssd_kernel,
        grid=(bsz, nc),
        in_specs=[pl.BlockSpec((SSD_CHUNK, SSD_CONV_DIM), lambda b, c: (b * nc + c, IN_XBC // SSD_CONV_DIM)),
                  pl.BlockSpec((SSD_CHUNK, SSD_INNER), lambda b, c: (b * nc + c, IN_Z // SSD_INNER)),
                  pl.BlockSpec((SSD_CHUNK, LANES), lambda b, c: (b * nc + c, IN_DT // LANES)),
                  const((SSD_CONV, SSD_CONV_DIM)), const((1, SSD_CONV_DIM)),
                  const((1, LANES)), const((1, LANES)), const((1, LANES)), const((1, SSD_INNER))],
        out_specs=pl.BlockSpec((SSD_CHUNK, SSD_INNER), lambda b, c: (b * nc + c, 0)),
        out_shape=jax.ShapeDtypeStruct((bsz * s, SSD_INNER), BF16),
        scratch_shapes=[pltpu.VMEM((SSD_CHUNK + 8, SSD_CONV_DIM), F32),
                        pltpu.VMEM((SSD_GROUPS, SSD_STATE, SSD_HPG * SSD_HEAD_DIM), F32)],
        compiler_params=_cparams(("parallel", "arbitrary"), 48),
        name="ssd_mixer",
    )(proj, proj, proj, conv_w.astype(F32), conv_b.reshape(1, -1).astype(F32),
      _pad_lanes(dt_bias), _pad_lanes(a_log), _pad_lanes(d_skip), ssd_norm.reshape(1, -1).astype(F32))


def _diff_prep_kernel(q_ref, k_ref, v_ref, qg_ref, kg_ref, cos_ref, sin_ref, q_out, k_out, v_out):
    cos, sin = cos_ref[...], sin_ref[...]
    scale = DIFF_HEAD_DIM ** -0.5
    for m in range(2 * DIFF_HEADS):
        lo = m * DIFF_HEAD_DIM
        q = _rms(q_ref[:, lo:lo + DIFF_HEAD_DIM], qg_ref[...]) * scale
        q_out[:, lo:lo + DIFF_HEAD_DIM] = _rope_seg(q, cos, sin).astype(BF16)
        k = _rms(k_ref[:, lo:lo + DIFF_HEAD_DIM], kg_ref[...])
        k_out[:, lo:lo + DIFF_HEAD_DIM] = _rope_seg(k, cos, sin).astype(BF16)
    v_out[...] = v_ref[...].astype(BF16)


def diff_prep(qkv, q_gain, k_gain, cos, sin, tm):
    t = qkv.shape[0]
    d = 2 * DIFF_HEADS * DIFF_HEAD_DIM
    const = lambda shape: pl.BlockSpec(shape, lambda i: (0, 0))
    return pl.pallas_call(
        _diff_prep_kernel,
        grid=(t // tm,),
        in_specs=[pl.BlockSpec((tm, d), lambda i: (i, 0)),
                  pl.BlockSpec((tm, d), lambda i: (i, 1)),
                  pl.BlockSpec((tm, d), lambda i: (i, 2)),
                  const((1, DIFF_HEAD_DIM)), const((1, DIFF_HEAD_DIM)),
                  pl.BlockSpec((tm, LANES), lambda i: (i, 0)),
                  pl.BlockSpec((tm, LANES), lambda i: (i, 0))],
        out_specs=[pl.BlockSpec((tm, d), lambda i: (i, 0))] * 3,
        out_shape=[jax.ShapeDtypeStruct((t, d), BF16)] * 3,
        compiler_params=_cparams(("parallel",), 56),
        name="diff_prep",
    )(qkv, qkv, qkv, q_gain.reshape(1, -1), k_gain.reshape(1, -1), cos, sin)


def _diff_attn_kernel(q_ref, k_ref, v_ref, lq1_ref, lk1_ref, lq2_ref, lk2_ref, sub_ref, o_ref,
                      m_ref, l_ref, acc_ref, *, tq, lam_init):
    dh = DIFF_HEAD_DIM
    chains = [(q_ref[:, c * dh:(c + 1) * dh], k_ref, c * dh, dh, v_ref, (c // 2) * DIFF_V, DIFF_V,
               m_ref.at[c], l_ref.at[c], acc_ref.at[c]) for c in range(2 * ATTN_HPS)]
    outs = _flash(chains, pl.program_id(2), tq, tq)
    lam = (jnp.exp(jnp.sum(lq1_ref[...] * lk1_ref[...], axis=-1, keepdims=True))
           - jnp.exp(jnp.sum(lq2_ref[...] * lk2_ref[...], axis=-1, keepdims=True)) + lam_init)
    for h in range(ATTN_HPS):
        attn = outs[2 * h] - lam * outs[2 * h + 1]
        o_ref[:, h * DIFF_V:(h + 1) * DIFF_V] = (_rms(attn, sub_ref[...]) * (1.0 - lam_init)).astype(o_ref.dtype)


def diff_attention(q, k, v, lam_q1, lam_k1, lam_q2, lam_k2, subln, lam_init, bsz, s, tq):
    d = 2 * DIFF_HEADS * DIFF_HEAD_DIM
    q, k, v = q.reshape(bsz, s, d), k.reshape(bsz, s, d), v.reshape(bsz, s, d)
    hps = ATTN_HPS
    const = lambda shape: pl.BlockSpec(shape, lambda b, h, i: (0, 0))
    out = pl.pallas_call(
        functools.partial(_diff_attn_kernel, tq=tq, lam_init=lam_init),
        grid=(bsz, DIFF_HEADS // hps, s // tq),
        in_specs=[pl.BlockSpec((None, tq, hps * DIFF_V), lambda b, h, i: (b, i, h)),
                  pl.BlockSpec((None, s, hps * DIFF_V), lambda b, h, i: (b, 0, h)),
                  pl.BlockSpec((None, s, hps * DIFF_V), lambda b, h, i: (b, 0, h)),
                  const((1, DIFF_HEAD_DIM)), const((1, DIFF_HEAD_DIM)),
                  const((1, DIFF_HEAD_DIM)), const((1, DIFF_HEAD_DIM)), const((1, DIFF_V))],
        out_specs=pl.BlockSpec((None, tq, hps * DIFF_V), lambda b, h, i: (b, i, h)),
        out_shape=jax.ShapeDtypeStruct((bsz, s, d), BF16),
        scratch_shapes=[pltpu.VMEM((2 * hps, tq, LANES), F32), pltpu.VMEM((2 * hps, tq, LANES), F32),
                        pltpu.VMEM((2 * hps, tq, DIFF_V), F32)],
        compiler_params=_cparams(("parallel", "parallel", "arbitrary"), 56),
        name="diff_attention",
    )(q, k, v, lam_q1.reshape(1, -1), lam_k1.reshape(1, -1), lam_q2.reshape(1, -1), lam_k2.reshape(1, -1),
      subln.reshape(1, -1))
    return out.reshape(bsz * s, d)


def _mem_attn_kernel(x_ref, g_ref, wq_ref, kv_ref, qg_ref, kg_ref, wo_ref, o_ref):
    x = x_ref[...]
    xn = _rms(x, g_ref[...]).astype(BF16)
    q = jnp.dot(xn, wq_ref[...], preferred_element_type=F32)
    kv = kv_ref[...]
    outs = []
    for h in range(MEM_HEADS):
        lo = h * MEM_HEAD_DIM
        qh = (_rms(q[:, lo:lo + MEM_HEAD_DIM], qg_ref[...]) * (MEM_HEAD_DIM ** -0.5)).astype(BF16)
        kh = _rms(kv[:, lo:lo + MEM_HEAD_DIM], kg_ref[...]).astype(BF16)
        vh = kv[:, MEM_WIDTH + lo:MEM_WIDTH + lo + MEM_HEAD_DIM].astype(BF16)
        s = _dot_nt(qh, kh)
        p = jnp.exp(s - jnp.max(s, axis=-1, keepdims=True))
        o = jnp.dot(p.astype(BF16), vh, preferred_element_type=F32)
        outs.append((o / jnp.sum(p, axis=-1, keepdims=True)).astype(BF16))
    o_ref[...] = x + jnp.dot(jnp.concatenate(outs, axis=1), wo_ref[...], preferred_element_type=F32)


def memory_attention(x, mem_kv, norm_g, w_q, q_gain, k_gain, w_o, bsz, s, tm):
    t, d = x.shape
    m = mem_kv.shape[0] // bsz
    nb = s // tm
    const = lambda shape: pl.BlockSpec(shape, lambda i: (0, 0))
    return pl.pallas_call(
        _mem_attn_kernel,
        grid=(t // tm,),
        in_specs=[pl.BlockSpec((tm, d), lambda i: (i, 0)),
                  const((1, d)), const((d, MEM_WIDTH)),
                  pl.BlockSpec((m, 2 * MEM_WIDTH), lambda i: (i // nb, 0)),
                  const((1, MEM_HEAD_DIM)), const((1, MEM_HEAD_DIM)), const((MEM_WIDTH, d))],
        out_specs=pl.BlockSpec((tm, d), lambda i: (i, 0)),
        out_shape=jax.ShapeDtypeStruct((t, d), F32),
        compiler_params=_cparams(("parallel",), 56),
        name="memory_attention",
    )(x, norm_g.reshape(1, d), w_q.astype(BF16), mem_kv, q_gain.reshape(1, -1), k_gain.reshape(1, -1),
      w_o.astype(BF16))


def _compare_exchange(lst, i, j):
    a, b = lst[i], lst[j]
    if b is None:
        return
    if a is None:
        lst[i], lst[j] = b, None
        return
    lst[i], lst[j] = jnp.maximum(a, b), jnp.minimum(a, b)


def _bitonic_sort(lst):
    n = len(lst)
    k = 2
    while k <= n:
        j = k // 2
        while j >= 1:
            for i in range(n):
                l = i ^ j
                if l > i:
                    if i & k == 0:
                        _compare_exchange(lst, i, l)
                    else:
                        _compare_exchange(lst, l, i)
            j //= 2
        k *= 2


def _bitonic_merge(lst):
    j = len(lst) // 2
    while j >= 1:
        for i in range(len(lst)):
            if i ^ j > i:
                _compare_exchange(lst, i, i ^ j)
        j //= 2


def _top_values(x, k):
    sub = 8
    nb = x.shape[0] // sub
    size = 1
    while size < nb:
        size *= 2
    lst = [x[i * sub:(i + 1) * sub, :] for i in range(nb)] + [None] * (size - nb)
    _bitonic_sort(lst)
    lst = (lst + [None] * k)[:k]
    shift = sub // 2
    while shift >= 1:
        other = [None if v is None else pltpu.roll(v, shift, 0) for v in lst]
        merged = []
        for i in range(k):
            a, b = lst[i], other[k - 1 - i]
            merged.append(b if a is None else a if b is None else jnp.maximum(a, b))
        _bitonic_merge(merged)
        lst = merged
        shift //= 2
    return [v[0:1, :] for v in lst]


_PAIRS = [(i, j) for i in range(PEER_TOPK) for j in range(PEER_TOPK) if (i + 1) * (j + 1) <= PEER_TOPK]
_PAIR_ROWS = -(-len(_PAIRS) // 8) * 8

ST_TAU, ST_A1, ST_B1, ST_ZINV = 0, PEER_HEADS, 2 * PEER_HEADS, 3 * PEER_HEADS


def _peer_route_kernel(x_ref, g_ref, wq_ref, sk_ref, xn_out, sc_out, st_out):
    tm = x_ref.shape[0]
    xn32 = _rms(x_ref[...], g_ref[...])
    xn_out[...] = xn32.T.astype(BF16)
    xn = xn32.astype(BF16)
    q = jnp.dot(xn, wq_ref[...], preferred_element_type=F32).astype(BF16)
    for g in range(2 * PEER_HEADS):
        sc_out[g * PEER_KEYS:(g + 1) * PEER_KEYS, :] = _dot_nt(sk_ref[g], q[:, g * PEER_HALF:(g + 1) * PEER_HALF])
    for h in range(PEER_HEADS):
        lo = 2 * h * PEER_KEYS
        a = _top_values(sc_out[lo:lo + PEER_KEYS, :], PEER_TOPK)
        b = _top_values(sc_out[lo + PEER_KEYS:lo + 2 * PEER_KEYS, :], PEER_TOPK)
        cands = [a[i] + b[j] for i, j in _PAIRS]
        cands += [jnp.full((1, tm), -jnp.inf, F32)] * (_PAIR_ROWS - len(_PAIRS))
        cand = jnp.concatenate(cands, axis=0)
        tau = _top_values(cand, PEER_TOPK)[-1]
        mx = a[0] + b[0]
        z = jnp.sum(jnp.where(cand >= tau, jnp.exp(cand - mx), 0.0), axis=0, keepdims=True)
        st_out[ST_TAU + h:ST_TAU + h + 1, :] = tau
        st_out[ST_A1 + h:ST_A1 + h + 1, :] = a[0]
        st_out[ST_B1 + h:ST_B1 + h + 1, :] = b[0]
        st_out[ST_ZINV + h:ST_ZINV + h + 1, :] = 1.0 / z


def peer_route(x, norm_g, w_q, sub_keys, tm):
    t, d = x.shape
    nq = 2 * PEER_HEADS * PEER_HALF
    sk = sub_keys.reshape(2 * PEER_HEADS, PEER_KEYS, PEER_HALF).astype(BF16)
    return pl.pallas_call(
        _peer_route_kernel,
        grid=(t // tm,),
        in_specs=[pl.BlockSpec((tm, d), lambda i: (i, 0)),
                  pl.BlockSpec((1, d), lambda i: (0, 0)),
                  pl.BlockSpec((d, nq), lambda i: (0, 0)),
                  pl.BlockSpec((2 * PEER_HEADS, PEER_KEYS, PEER_HALF), lambda i: (0, 0, 0))],
        out_specs=[pl.BlockSpec((d, tm), lambda i: (0, i)),
                   pl.BlockSpec((2 * PEER_HEADS * PEER_KEYS, tm), lambda i: (0, i)),
                   pl.BlockSpec((4 * PEER_HEADS, tm), lambda i: (0, i))],
        out_shape=[jax.ShapeDtypeStruct((d, t), BF16),
                   jax.ShapeDtypeStruct((2 * PEER_HEADS * PEER_KEYS, t), F32),
                   jax.ShapeDtypeStruct((4 * PEER_HEADS, t), F32)],
        compiler_params=_cparams(("parallel",), 56),
        name="peer_route",
    )(x, norm_g.reshape(1, d), w_q.astype(BF16), sk)


def _gelu(x):
    return 0.5 * x * (1.0 + lax.erf(x * (2.0 ** -0.5)))


def _peer_gated_act(blk, half, js, nr, xu_ref, w_ref, sc_ref, st_ref, e0_ref, e1_ref):
    K = PEER_KEYS
    tmh = xu_ref.shape[1]
    for j in js:
        r = jnp.minimum(blk * nr + j, K - 1)
        s0_rows = [sc_ref[pl.ds(2 * h * K + r, 1), :] for h in range(PEER_HEADS)]
        e0_rows = [e0_ref[h, pl.ds(r, 1), :] for h in range(PEER_HEADS)]
        for c in range(tmh // LANES):
            cl = slice(c * LANES, (c + 1) * LANES)
            cs = slice(half * tmh + c * LANES, half * tmh + (c + 1) * LANES)
            gate = None
            for h in range(PEER_HEADS):
                lo = 2 * h * K
                s = sc_ref[lo + K:lo + 2 * K, cs] + s0_rows[h][:, cs]
                part = jnp.where(s >= st_ref[ST_TAU + h:ST_TAU + h + 1, cs],
                                 e1_ref[h, :, cs] * e0_rows[h][:, cs], 0.0)
                gate = part if gate is None else gate + part
            w_ref[j * K:(j + 1) * K, cl] = (_gelu(xu_ref[j * K:(j + 1) * K, cl]) * gate).astype(BF16)


def _peer_dense_kernel(xnt_ref, sc_ref, st_ref, u_ref, vta_ref, vtb_ref, o_ref, xu0_ref, xu1_ref, w0_ref, w1_ref,
                       e0_ref, e1_ref, *, nr):
    g = pl.program_id(1)
    last = pl.num_programs(1) - 1
    K = PEER_KEYS
    eb = nr * K
    tmh = xnt_ref.shape[1] // 2
    xu, w = (xu0_ref, xu1_ref), (w0_ref, w1_ref)

    def score(s):
        hd = s % 2
        xu[hd][...] = jnp.dot(u_ref[(s // 2) * eb:(s // 2 + 1) * eb, :], xnt_ref[:, hd * tmh:(hd + 1) * tmh],
                              preferred_element_type=F32)

    def value(s):
        hd = s % 2
        vt_ref = vtb_ref if s < 2 else vta_ref
        o_ref[:, hd * tmh:(hd + 1) * tmh] += jnp.dot(vt_ref[...], w[hd][...], preferred_element_type=F32)

    def act(s, js):
        hv = (s + 1) % 2
        _peer_gated_act(2 * g + (s - 1) // 2, hv, js, nr, xu[hv], w[hv], sc_ref, st_ref, e0_ref, e1_ref)

    def slot(s):
        score(s)
        act(s, range(0, nr // 2))
        value(s)
        act(s, range(nr // 2, nr))

    @pl.when(g == 0)
    def _():
        o_ref[...] = jnp.zeros(o_ref.shape, F32)
        for h in range(PEER_HEADS):
            lo = 2 * h * K
            e0_ref[h] = jnp.exp(sc_ref[lo:lo + K, :] - st_ref[ST_A1 + h:ST_A1 + h + 1, :])
            e1_ref[h] = (jnp.exp(sc_ref[lo + K:lo + 2 * K, :] - st_ref[ST_B1 + h:ST_B1 + h + 1, :])
                         * st_ref[ST_ZINV + h:ST_ZINV + h + 1, :])
        score(0)
        score(1)
        act(1, range(nr))

    @pl.when(g > 0)
    def _():
        slot(0)
        slot(1)

    @pl.when(g < last)
    def _():
        slot(2)
        slot(3)


def peer_dense(xnt, sc_t, stats, u_tab, v_tab, tm, eb):
    d, t = xnt.shape
    n_exp = u_tab.shape[0]
    nblk = n_exp // eb
    u16 = u_tab.astype(BF16)
    vt16 = v_tab.astype(BF16).T
    nsc = sc_t.shape[0]
    return pl.pallas_call(
        functools.partial(_peer_dense_kernel, nr=eb // PEER_KEYS),
        grid=(t // tm, nblk // 2 + 1),
        in_specs=[pl.BlockSpec((d, tm), lambda i, g: (0, i)),
                  pl.BlockSpec((nsc, tm), lambda i, g: (0, i)),
                  pl.BlockSpec((4 * PEER_HEADS, tm), lambda i, g: (0, i)),
                  pl.BlockSpec((2 * eb, d), lambda i, g: (jnp.minimum(g, nblk // 2 - 1), 0)),
                  pl.BlockSpec((d, eb), lambda i, g: (0, jnp.minimum(2 * g, nblk - 1))),
                  pl.BlockSpec((d, eb), lambda i, g: (0, jnp.maximum(2 * g - 1, 0)))],
        out_specs=pl.BlockSpec((d, tm), lambda i, g: (0, i)),
        out_shape=jax.ShapeDtypeStruct((d, t), F32),
        scratch_shapes=[pltpu.VMEM((eb, tm // 2), F32),
                        pltpu.VMEM((eb, tm // 2), F32),
                        pltpu.VMEM((eb, tm // 2), BF16),
                        pltpu.VMEM((eb, tm // 2), BF16),
                        pltpu.VMEM((PEER_HEADS, PEER_KEYS, tm), F32),
                        pltpu.VMEM((PEER_HEADS, PEER_KEYS, tm), F32)],
        compiler_params=_cparams(("parallel", "arbitrary"), 56),
        name="peer_dense",
    )(xnt, sc_t, stats, u16, vt16, vt16)


def _transpose_add_kernel(x_ref, pt_ref, o_ref):
    o_ref[...] = x_ref[...] + pt_ref[...].T


def transpose_add(x, pt, tm):
    t, d = x.shape
    return pl.pallas_call(
        _transpose_add_kernel,
        grid=(t // tm,),
        in_specs=[pl.BlockSpec((tm, d), lambda i: (i, 0)), pl.BlockSpec((d, tm), lambda i: (0, i))],
        out_specs=pl.BlockSpec((tm, d), lambda i: (i, 0)),
        out_shape=jax.ShapeDtypeStruct((t, d), F32),
        compiler_params=_cparams(("parallel",), 48),
        name="transpose_add",
    )(x, pt)


def _even_w_in(w_in):
    o = np.cumsum([0, MLA_Q_RANK, MLA_KV_RANK, MLA_ROPE, SSD_INNER, SSD_CONV_DIM, SSD_HEADS])
    c_q, c_kv, k_r, z, xbc, dt = (w_in[:, o[i]:o[i + 1]] for i in range(6))
    dt = jnp.pad(dt, ((0, 0), (0, LANES - SSD_HEADS)))
    return jnp.concatenate([xbc, z, c_q, c_kv, _pad_rope_cols(k_r), dt], axis=1).astype(BF16)


def _tile(n, pref):
    return pref if n % pref == 0 else n


def kernel(x, mem, positions, norm_mix, norm_mem, norm_mem_kv, norm_ffn, mem_w_q, mem_w_kv, mem_q_gain, mem_k_gain, mem_w_o, peer_w_q, peer_sub_keys, peer_u, peer_v, ev_w_in, ev_q_norm, ev_kv_norm, ev_w_uq, ev_w_ukv, ev_q_gain, ev_k_gain, ev_conv_w, ev_conv_b, ev_dt_bias, ev_a_log, ev_d_skip, ev_ssd_norm, ev_w_out, od_w_qkv, od_q_gain, od_k_gain, od_lam_q1, od_lam_k1, od_lam_q2, od_lam_k2, od_subln, od_w_o):
    bsz, s, d = x.shape
    t = bsz * s
    depth = norm_mix.shape[0]
    tm = _tile(s, 512)
    tq = _tile(s, 512)
    xf = x.reshape(t, d)
    pos = positions.reshape(t)
    mem_f = mem.reshape(-1, d)
    cos_mla, sin_mla = rope_tables(pos, MLA_ROPE // 2, LANES // 2)
    cos_dif, sin_dif = rope_tables(pos, DIFF_HEAD_DIM // 2, LANES // 2)

    for layer in range(depth):
        i = layer // 2
        if layer % 2 == 0:
            proj = norm_matmul(xf, norm_mix[layer], _even_w_in(ev_w_in[i]), tm, IN_TOTAL // 2)
            q, k, v = mla_prep(proj, ev_q_norm[i], ev_kv_norm[i], ev_w_uq[i], ev_w_ukv[i],
                               ev_q_gain[i], ev_k_gain[i], cos_mla, sin_mla, tm)
            mla_out = mla_attention(q, k, v, bsz, s, tq)
            y = ssd_mixer(proj, ev_conv_w[i], ev_conv_b[i], ev_dt_bias[i], ev_a_log[i], ev_d_skip[i],
                          ev_ssd_norm[i], bsz, s)
            w_out = ev_w_out[i].astype(BF16)
            n_mla = MLA_HEADS * MLA_V
            xf = matmul_residual([mla_out, y], [w_out[:n_mla], w_out[n_mla:]], xf, tm, 1024)
        else:
            qkv = norm_matmul(xf, norm_mix[layer], od_w_qkv[i].astype(BF16), tm, 1024)
            q, k, v = diff_prep(qkv, od_q_gain[i], od_k_gain[i], cos_dif, sin_dif, tm)
            lam_init = 0.8 - 0.6 * math.exp(-0.3 * layer)
            attn = diff_attention(q, k, v, od_lam_q1[i], od_lam_k1[i], od_lam_q2[i], od_lam_k2[i],
                                  od_subln[i], lam_init, bsz, s, tq)
            xf = matmul_residual([attn], [od_w_o[i].astype(BF16)], xf, tm, 1024)
        mem_kv = norm_matmul(mem_f, norm_mem_kv[layer], mem_w_kv[layer].astype(BF16),
                             _tile(mem_f.shape[0], 512), 2 * MEM_WIDTH)
        xf = memory_attention(xf, mem_kv, norm_mem[layer], mem_w_q[layer], mem_q_gain[layer],
                              mem_k_gain[layer], mem_w_o[layer], bsz, s, tm)
        xnt, sc_t, stats = peer_route(xf, norm_ffn[layer], peer_w_q[layer], peer_sub_keys[layer], tm)
        xf = transpose_add(xf, peer_dense(xnt, sc_t, stats, peer_u[layer], peer_v[layer], tm, 512), tm)
    return xf.reshape(bsz, s, d)
```

```python
import functools
import math

import jax
import jax.numpy as jnp
import numpy as np
from jax import lax
from jax.experimental import pallas as pl
from jax.experimental.pallas import tpu as pltpu

F32 = jnp.float32
BF16 = jnp.bfloat16

EPS = 1e-6
ROPE_THETA = 10000.0
LANES = 128

MLA_HEADS = 8
MLA_Q_RANK = 512
MLA_KV_RANK = 512
MLA_NOPE = 128
MLA_ROPE = 64
MLA_V = 128
MLA_QK = MLA_NOPE + MLA_ROPE
MLA_QK_PAD = 256

SSD_HEADS = 16
SSD_HEAD_DIM = 64
SSD_INNER = SSD_HEADS * SSD_HEAD_DIM
SSD_GROUPS = 4
SSD_STATE = 128
SSD_CONV = 4
SSD_CHUNK = 128
SSD_CONV_DIM = SSD_INNER + 2 * SSD_GROUPS * SSD_STATE
SSD_HPG = SSD_HEADS // SSD_GROUPS

DIFF_HEADS = 8
DIFF_HEAD_DIM = 128
DIFF_V = 256

MEM_HEADS = 4
MEM_HEAD_DIM = 128
MEM_WIDTH = MEM_HEADS * MEM_HEAD_DIM

PEER_HEADS = 8
PEER_KEYS = 128
PEER_TOPK = 16
PEER_HALF = 128

IN_XBC = 0
IN_Z = IN_XBC + SSD_CONV_DIM
IN_CQ = IN_Z + SSD_INNER
IN_CKV = IN_CQ + MLA_Q_RANK
IN_KR = IN_CKV + MLA_KV_RANK
IN_DT = IN_KR + LANES
IN_TOTAL = IN_DT + LANES

NEG_BIG = -1e30


def _cparams(sem, vmem_mb):
    return pltpu.CompilerParams(dimension_semantics=sem, vmem_limit_bytes=vmem_mb * 1024 * 1024)


def _rms(x, g, n=None):
    ss = jnp.sum(x * x, axis=-1, keepdims=True)
    n = x.shape[-1] if n is None else n
    return x * lax.rsqrt(ss * (1.0 / n) + EPS) * g


def _dot_nt(a, b):
    return lax.dot_general(a, b, (((1,), (1,)), ((), ())), preferred_element_type=F32)


def _norm_mm_kernel(x_ref, g_ref, w_ref, o_ref, xn_ref):
    @pl.when(pl.program_id(1) == 0)
    def _():
        xn_ref[...] = _rms(x_ref[...], g_ref[...]).astype(BF16)

    o_ref[...] = jnp.dot(xn_ref[...], w_ref[...], preferred_element_type=F32).astype(o_ref.dtype)


def norm_matmul(x, g, w, tm, tn, out_dtype=F32):
    t, d = x.shape
    n = w.shape[1]
    return pl.pallas_call(
        _norm_mm_kernel,
        grid=(t // tm, n // tn),
        in_specs=[pl.BlockSpec((tm, d), lambda i, j: (i, 0)),
                  pl.BlockSpec((1, d), lambda i, j: (0, 0)),
                  pl.BlockSpec((d, tn), lambda i, j: (0, j))],
        out_specs=pl.BlockSpec((tm, tn), lambda i, j: (i, j)),
        out_shape=jax.ShapeDtypeStruct((t, n), out_dtype),
        scratch_shapes=[pltpu.VMEM((tm, d), BF16)],
        compiler_params=_cparams(("parallel", "arbitrary"), 56),
        name="norm_matmul",
    )(x, g.reshape(1, d), w)


def _mm_res_kernel(*refs, n_a):
    res_ref, o_ref = refs[2 * n_a], refs[2 * n_a + 1]
    acc = res_ref[...]
    for a_ref, w_ref in zip(refs[:n_a], refs[n_a:2 * n_a]):
        acc = acc + jnp.dot(a_ref[...], w_ref[...], preferred_element_type=F32)
    o_ref[...] = acc


def matmul_residual(a_list, w_list, res, tm, tn):
    t, n = res.shape
    n_a = len(a_list)
    in_specs = ([pl.BlockSpec((tm, a.shape[1]), lambda i, j: (i, 0)) for a in a_list]
                + [pl.BlockSpec((w.shape[0], tn), lambda i, j: (0, j)) for w in w_list]
                + [pl.BlockSpec((tm, tn), lambda i, j: (i, j))])
    return pl.pallas_call(
        functools.partial(_mm_res_kernel, n_a=n_a),
        grid=(t // tm, n // tn),
        in_specs=in_specs,
        out_specs=pl.BlockSpec((tm, tn), lambda i, j: (i, j)),
        out_shape=jax.ShapeDtypeStruct((t, n), F32),
        compiler_params=_cparams(("parallel", "parallel"), 56),
        name="matmul_residual",
    )(*a_list, *w_list, res)


def _rope_tab_kernel(pos_ref, inv_ref, sgn_ref, cos_ref, sin_ref):
    ang = pos_ref[...].astype(F32) * inv_ref[...]
    cos_ref[...] = jnp.cos(ang)
    sin_ref[...] = jnp.sin(ang) * sgn_ref[...]


def rope_tables(pos, half, stride):
    t = pos.shape[0]
    lane = np.arange(LANES)
    j = lane % stride
    inv = np.where(j < half, 1.0 / (ROPE_THETA ** (np.minimum(j, half - 1).astype(np.float32) / half)), 0.0)
    sgn = np.where(lane < stride, -1.0, 1.0)
    tm = min(t, 1024)
    return pl.pallas_call(
        _rope_tab_kernel,
        grid=(t // tm,),
        in_specs=[pl.BlockSpec((tm, 1), lambda i: (i, 0)),
                  pl.BlockSpec((1, LANES), lambda i: (0, 0)),
                  pl.BlockSpec((1, LANES), lambda i: (0, 0))],
        out_specs=[pl.BlockSpec((tm, LANES), lambda i: (i, 0))] * 2,
        out_shape=[jax.ShapeDtypeStruct((t, LANES), F32)] * 2,
        compiler_params=_cparams(("parallel",), 32),
        name="rope_tables",
    )(pos.reshape(t, 1), jnp.asarray(inv, F32).reshape(1, LANES), jnp.asarray(sgn, F32).reshape(1, LANES))


def _rope_seg(x, cos, sin):
    return x * cos + pltpu.roll(x, LANES // 2, 1) * sin


def _rep(x, n):
    return x if n == 1 else jnp.concatenate([x] * n, axis=1)


def _flash(chains, n_full, tq, tk):
    for (_, _, _, _, _, _, _, m_ref, l_ref, acc_ref) in chains:
        m_ref[...] = jnp.full(m_ref.shape, NEG_BIG, F32)
        l_ref[...] = jnp.zeros(l_ref.shape, F32)
        acc_ref[...] = jnp.zeros(acc_ref.shape, F32)

    def step(j, masked):
        start = pl.multiple_of(j * tk, tk)
        if masked:
            row = lax.broadcasted_iota(jnp.int32, (tq, tk), 0)
            col = lax.broadcasted_iota(jnp.int32, (tq, tk), 1)
            keep = col <= row
        for (q, k_ref, koff, dk, v_ref, voff, dv, m_ref, l_ref, acc_ref) in chains:
            s = _dot_nt(q, k_ref[pl.ds(start, tk), koff:koff + dk])
            if masked:
                s = jnp.where(keep, s, NEG_BIG)
            m_prev = m_ref[...]
            m_new = jnp.maximum(m_prev, jnp.max(s, axis=-1, keepdims=True))
            p = jnp.exp(s - _rep(m_new, tk // LANES))
            alpha = jnp.exp(m_prev - m_new)
            l_ref[...] = alpha * l_ref[...] + jnp.sum(p, axis=-1, keepdims=True)
            acc_ref[...] = (_rep(alpha, dv // LANES) * acc_ref[...]
                            + jnp.dot(p.astype(BF16), v_ref[pl.ds(start, tk), voff:voff + dv],
                                      preferred_element_type=F32))
            m_ref[...] = m_new

    def body(j, carry):
        step(j, False)
        return carry

    lax.fori_loop(0, n_full, body, 0)
    step(n_full, True)
    return [acc_ref[...] / _rep(l_ref[...], dv // LANES) for (_, _, _, _, _, _, dv, _, l_ref, acc_ref) in chains]


def _mla_prep_kernel(cq_ref, ckv_ref, kr_ref, qn_ref, kvn_ref, wuq_ref, wukv_ref, qg_ref, kg_ref,
                     cos_ref, sin_ref, q_out, k_out, v_out):
    cq = _rms(cq_ref[...], qn_ref[...]).astype(BF16)
    qf = jnp.dot(cq, wuq_ref[...], preferred_element_type=F32)
    ckv = _rms(ckv_ref[...], kvn_ref[...]).astype(BF16)
    kvf = jnp.dot(ckv, wukv_ref[...], preferred_element_type=F32)
    kr = kr_ref[...]
    kr_ss = jnp.sum(kr * kr, axis=-1, keepdims=True)
    cos, sin = cos_ref[...], sin_ref[...]
    qg, kg = qg_ref[...], kg_ref[...]
    scale = MLA_QK ** -0.5
    for h in range(MLA_HEADS):
        lo = h * MLA_QK_PAD
        qh = _rms(qf[:, lo:lo + MLA_QK_PAD], qg, MLA_QK) * scale
        q_out[:, lo:lo + LANES] = qh[:, :LANES].astype(BF16)
        q_out[:, lo + LANES:lo + 2 * LANES] = _rope_seg(qh[:, LANES:], cos, sin).astype(BF16)
        kn = kvf[:, lo:lo + MLA_NOPE]
        r = lax.rsqrt((jnp.sum(kn * kn, axis=-1, keepdims=True) + kr_ss) * (1.0 / MLA_QK) + EPS)
        k_out[:, lo:lo + LANES] = (kn * r * kg[:, :LANES]).astype(BF16)
        k_out[:, lo + LANES:lo + 2 * LANES] = _rope_seg(kr * r * kg[:, LANES:], cos, sin).astype(BF16)
        v_out[:, h * MLA_V:(h + 1) * MLA_V] = kvf[:, lo + MLA_NOPE:lo + MLA_QK_PAD].astype(BF16)


def _pad_rope_cols(w):
    half = MLA_ROPE // 2
    z = jnp.zeros(w.shape[:-1] + (half,), w.dtype)
    return jnp.concatenate([w[..., :half], z, w[..., half:], z], axis=-1)


def mla_prep(proj, q_norm, kv_norm, w_uq, w_ukv, q_gain, k_gain, cos, sin, tm):
    t = proj.shape[0]
    hq = MLA_HEADS * MLA_QK_PAD
    wq = w_uq.reshape(MLA_Q_RANK, MLA_HEADS, MLA_QK)
    wq = jnp.concatenate([wq[..., :MLA_NOPE], _pad_rope_cols(wq[..., MLA_NOPE:])], axis=-1)
    wq = wq.reshape(MLA_Q_RANK, hq).astype(BF16)
    qg = jnp.concatenate([q_gain[:MLA_NOPE], _pad_rope_cols(q_gain[MLA_NOPE:])]).reshape(1, MLA_QK_PAD)
    kg = jnp.concatenate([k_gain[:MLA_NOPE], _pad_rope_cols(k_gain[MLA_NOPE:])]).reshape(1, MLA_QK_PAD)
    const = lambda shape: pl.BlockSpec(shape, lambda i: (0, 0))
    return pl.pallas_call(
        _mla_prep_kernel,
        grid=(t // tm,),
        in_specs=[pl.BlockSpec((tm, MLA_Q_RANK), lambda i: (i, IN_CQ // MLA_Q_RANK)),
                  pl.BlockSpec((tm, MLA_KV_RANK), lambda i: (i, IN_CKV // MLA_KV_RANK)),
                  pl.BlockSpec((tm, LANES), lambda i: (i, IN_KR // LANES)),
                  const((1, MLA_Q_RANK)), const((1, MLA_KV_RANK)),
                  const((MLA_Q_RANK, hq)), const((MLA_KV_RANK, hq)),
                  const((1, MLA_QK_PAD)), const((1, MLA_QK_PAD)),
                  pl.BlockSpec((tm, LANES), lambda i: (i, 0)),
                  pl.BlockSpec((tm, LANES), lambda i: (i, 0))],
        out_specs=[pl.BlockSpec((tm, hq), lambda i: (i, 0)),
                   pl.BlockSpec((tm, hq), lambda i: (i, 0)),
                   pl.BlockSpec((tm, MLA_HEADS * MLA_V), lambda i: (i, 0))],
        out_shape=[jax.ShapeDtypeStruct((t, hq), BF16), jax.ShapeDtypeStruct((t, hq), BF16),
                   jax.ShapeDtypeStruct((t, MLA_HEADS * MLA_V), BF16)],
        compiler_params=_cparams(("parallel",), 56),
        name="mla_prep",
    )(proj, proj, proj, q_norm.reshape(1, -1), kv_norm.reshape(1, -1), wq, w_ukv.astype(BF16), qg, kg, cos, sin)


MLA_HPS = 4
DIFF_HPS = 2


def _mla_attn_kernel(q_ref, k_ref, v_ref, o_ref, m_ref, l_ref, acc_ref, *, tq):
    chains = [(q_ref[:, h * MLA_QK_PAD:(h + 1) * MLA_QK_PAD], k_ref, h * MLA_QK_PAD, MLA_QK_PAD,
               v_ref, h * MLA_V, MLA_V, m_ref.at[h], l_ref.at[h], acc_ref.at[h]) for h in range(MLA_HPS)]
    outs = _flash(chains, pl.program_id(2), tq, tq)
    for h in range(MLA_HPS):
        o_ref[:, h * MLA_V:(h + 1) * MLA_V] = outs[h].astype(o_ref.dtype)


def mla_attention(q, k, v, bsz, s, tq):
    hq = MLA_HEADS * MLA_QK_PAD
    q, k = q.reshape(bsz, s, hq), k.reshape(bsz, s, hq)
    v = v.reshape(bsz, s, MLA_HEADS * MLA_V)
    hps = MLA_HPS
    out = pl.pallas_call(
        functools.partial(_mla_attn_kernel, tq=tq),
        grid=(bsz, MLA_HEADS // hps, s // tq),
        in_specs=[pl.BlockSpec((None, tq, hps * MLA_QK_PAD), lambda b, h, i: (b, i, h)),
                  pl.BlockSpec((None, s, hps * MLA_QK_PAD), lambda b, h, i: (b, 0, h)),
                  pl.BlockSpec((None, s, hps * MLA_V), lambda b, h, i: (b, 0, h))],
        out_specs=pl.BlockSpec((None, tq, hps * MLA_V), lambda b, h, i: (b, i, h)),
        out_shape=jax.ShapeDtypeStruct((bsz, s, MLA_HEADS * MLA_V), BF16),
        scratch_shapes=[pltpu.VMEM((hps, tq, LANES), F32), pltpu.VMEM((hps, tq, LANES), F32),
                        pltpu.VMEM((hps, tq, MLA_V), F32)],
        compiler_params=_cparams(("parallel", "parallel", "arbitrary"), 48),
        name="mla_attention",
    )(q, k, v)
    return out.reshape(bsz * s, MLA_HEADS * MLA_V)


def _silu(x):
    return x * (1.0 / (1.0 + jnp.exp(-x)))


def _softplus(x):
    return jnp.maximum(x, 0.0) + jnp.log(1.0 + jnp.exp(-jnp.abs(x)))


def _ssd_kernel(xbc_ref, z_ref, dt_ref, cw_ref, cb_ref, dtb_ref, alog_ref, dsk_ref, nrm_ref, y_ref,
                xc_ref, st_ref):
    c = pl.program_id(1)
    L, P, N, G, HPG = SSD_CHUNK, SSD_HEAD_DIM, SSD_STATE, SSD_GROUPS, SSD_HPG
    TAIL = 8

    @pl.when(c == 0)
    def _():
        xc_ref[0:TAIL, :] = jnp.zeros((TAIL, SSD_CONV_DIM), F32)
        st_ref[...] = jnp.zeros(st_ref.shape, F32)

    xc_ref[TAIL:TAIL + L, :] = xbc_ref[...]
    conv = cb_ref[...] + jnp.zeros((L, SSD_CONV_DIM), F32)
    for k in range(SSD_CONV):
        off = TAIL - (SSD_CONV - 1) + k
        conv = conv + xc_ref[off:off + L, :] * cw_ref[k:k + 1, :]
    xc_ref[0:TAIL, :] = xc_ref[L:L + TAIL, :]
    xbc = _silu(conv)

    dt = _softplus(dt_ref[...] + dtb_ref[...])
    adt = dt * (-jnp.exp(alog_ref[...]))
    li = lax.broadcasted_iota(jnp.int32, (L, L), 0)
    si = lax.broadcasted_iota(jnp.int32, (L, L), 1)
    tri = (si <= li).astype(F32)
    a_cs = jnp.dot(tri, adt, preferred_element_type=F32, precision=lax.Precision.HIGHEST)
    a_cs_t = a_cs.T
    a_last = a_cs[L - 1:L, :]
    dec_state = jnp.exp(a_last - a_cs)
    dec_out = jnp.exp(a_cs)
    dec_chunk = jnp.exp(a_last)
    causal = si <= li

    zs = _silu(z_ref[...])
    for g in range(G):
        bg = xbc[:, SSD_INNER + g * N:SSD_INNER + (g + 1) * N]
        cg = xbc[:, SSD_INNER + G * N + g * N:SSD_INNER + G * N + (g + 1) * N]
        bg16, cg16 = bg.astype(BF16), cg.astype(BF16)
        cb = _dot_nt(cg16, bg16)
        bgt16 = bg.T.astype(BF16)
        xs_g = xbc[:, g * HPG * P:(g + 1) * HPG * P]
        xdd, ydiag = [], []
        for hh in range(HPG):
            h = g * HPG + hh
            xs = xs_g[:, hh * P:(hh + 1) * P]
            xd = xs * dt[:, h:h + 1]
            seg = a_cs[:, h:h + 1] - a_cs_t[h:h + 1, :]
            lm = cb * jnp.exp(jnp.where(causal, seg, NEG_BIG))
            ydiag.append(jnp.dot(lm.astype(BF16), xd.astype(BF16), preferred_element_type=F32))
            xdd.append((xd * dec_state[:, h:h + 1]).astype(BF16))
        prev = st_ref[g]
        yoff = jnp.dot(cg16, prev.astype(BF16), preferred_element_type=F32)
        new = jnp.dot(bgt16, jnp.concatenate(xdd, axis=1), preferred_element_type=F32)
        ys, decs = [], []
        for hh in range(HPG):
            h = g * HPG + hh
            xs = xs_g[:, hh * P:(hh + 1) * P]
            y = ydiag[hh] + yoff[:, hh * P:(hh + 1) * P] * dec_out[:, h:h + 1] + dsk_ref[:, h:h + 1] * xs
            ys.append(y)
            decs.append(jnp.broadcast_to(dec_chunk[:, h:h + 1], (N, P)))
        st_ref[g] = prev * jnp.concatenate(decs, axis=1) + new
        yg = jnp.concatenate(ys, axis=1) * zs[:, g * HPG * P:(g + 1) * HPG * P]
        y_ref[:, g * HPG * P:(g + 1) * HPG * P] = _rms(yg, nrm_ref[:, g * HPG * P:(g + 1) * HPG * P]).astype(y_ref.dtype)


def _pad_lanes(v):
    return jnp.pad(v.astype(F32), (0, LANES - v.shape[0])).reshape(1, LANES)


def ssd_mixer(proj, conv_w, conv_b, dt_bias, a_log, d_skip, ssd_norm, bsz, s):
    nc = s // SSD_CHUNK
    const = lambda shape: pl.BlockSpec(shape, lambda b, c: (0, 0))
    return pl.pallas_call(
        _ssd_kernel,
        grid=(bsz, nc),
        in_specs=[pl.BlockSpec((SSD_CHUNK, SSD_CONV_DIM), lambda b, c: (b * nc + c, IN_XBC // SSD_CONV_DIM)),
                  pl.BlockSpec((SSD_CHUNK, SSD_INNER), lambda b, c: (b * nc + c, IN_Z // SSD_INNER)),
                  pl.BlockSpec((SSD_CHUNK, LANES), lambda b, c: (b * nc + c, IN_DT // LANES)),
                  const((SSD_CONV, SSD_CONV_DIM)), const((1, SSD_CONV_DIM)),
                  const((1, LANES)), const((1, LANES)), const((1, LANES)), const((1, SSD_INNER))],
        out_specs=pl.BlockSpec((SSD_CHUNK, SSD_INNER), lambda b, c: (b * nc + c, 0)),
        out_shape=jax.ShapeDtypeStruct((bsz * s, SSD_INNER), BF16),
        scratch_shapes=[pltpu.VMEM((SSD_CHUNK + 8, SSD_CONV_DIM), F32),
                        pltpu.VMEM((SSD_GROUPS, SSD_STATE, SSD_HPG * SSD_HEAD_DIM), F32)],
        compiler_params=_cparams(("parallel", "arbitrary"), 48),
        name="ssd_mixer",
    )(proj, proj, proj, conv_w.astype(F32), conv_b.reshape(1, -1).astype(F32),
      _pad_lanes(dt_bias), _pad_lanes(a_log), _pad_lanes(d_skip), ssd_norm.reshape(1, -1).astype(F32))


def _diff_prep_kernel(q_ref, k_ref, v_ref, qg_ref, kg_ref, cos_ref, sin_ref, q_out, k_out, v_out):
    cos, sin = cos_ref[...], sin_ref[...]
    scale = DIFF_HEAD_DIM ** -0.5
    for m in range(2 * DIFF_HEADS):
        lo = m * DIFF_HEAD_DIM
        q = _rms(q_ref[:, lo:lo + DIFF_HEAD_DIM], qg_ref[...]) * scale
        q_out[:, lo:lo + DIFF_HEAD_DIM] = _rope_seg(q, cos, sin).astype(BF16)
        k = _rms(k_ref[:, lo:lo + DIFF_HEAD_DIM], kg_ref[...])
        k_out[:, lo:lo + DIFF_HEAD_DIM] = _rope_seg(k, cos, sin).astype(BF16)
    v_out[...] = v_ref[...].astype(BF16)


def diff_prep(qkv, q_gain, k_gain, cos, sin, tm):
    t = qkv.shape[0]
    d = 2 * DIFF_HEADS * DIFF_HEAD_DIM
    const = lambda shape: pl.BlockSpec(shape, lambda i: (0, 0))
    return pl.pallas_call(
        _diff_prep_kernel,
        grid=(t // tm,),
        in_specs=[pl.BlockSpec((tm, d), lambda i: (i, 0)),
                  pl.BlockSpec((tm, d), lambda i: (i, 1)),
                  pl.BlockSpec((tm, d), lambda i: (i, 2)),
                  const((1, DIFF_HEAD_DIM)), const((1, DIFF_HEAD_DIM)),
                  pl.BlockSpec((tm, LANES), lambda i: (i, 0)),
                  pl.BlockSpec((tm, LANES), lambda i: (i, 0))],
        out_specs=[pl.BlockSpec((tm, d), lambda i: (i, 0))] * 3,
        out_shape=[jax.ShapeDtypeStruct((t, d), BF16)] * 3,
        compiler_params=_cparams(("parallel",), 56),
        name="diff_prep",
    )(qkv, qkv, qkv, q_gain.reshape(1, -1), k_gain.reshape(1, -1), cos, sin)


def _diff_attn_kernel(q_ref, k_ref, v_ref, lq1_ref, lk1_ref, lq2_ref, lk2_ref, sub_ref, o_ref,
                      m_ref, l_ref, acc_ref, *, tq, lam_init):
    dh = DIFF_HEAD_DIM
    chains = [(q_ref[:, c * dh:(c + 1) * dh], k_ref, c * dh, dh, v_ref, (c // 2) * DIFF_V, DIFF_V,
               m_ref.at[c], l_ref.at[c], acc_ref.at[c]) for c in range(2 * DIFF_HPS)]
    outs = _flash(chains, pl.program_id(2), tq, tq)
    lam = (jnp.exp(jnp.sum(lq1_ref[...] * lk1_ref[...], axis=-1, keepdims=True))
           - jnp.exp(jnp.sum(lq2_ref[...] * lk2_ref[...], axis=-1, keepdims=True)) + lam_init)
    for h in range(DIFF_HPS):
        attn = outs[2 * h] - lam * outs[2 * h + 1]
        o_ref[:, h * DIFF_V:(h + 1) * DIFF_V] = (_rms(attn, sub_ref[...]) * (1.0 - lam_init)).astype(o_ref.dtype)


def diff_attention(q, k, v, lam_q1, lam_k1, lam_q2, lam_k2, subln, lam_init, bsz, s, tq):
    d = 2 * DIFF_HEADS * DIFF_HEAD_DIM
    q, k, v = q.reshape(bsz, s, d), k.reshape(bsz, s, d), v.reshape(bsz, s, d)
    hps = DIFF_HPS
    const = lambda shape: pl.BlockSpec(shape, lambda b, h, i: (0, 0))
    out = pl.pallas_call(
        functools.partial(_diff_attn_kernel, tq=tq, lam_init=lam_init),
        grid=(bsz, DIFF_HEADS // hps, s // tq),
        in_specs=[pl.BlockSpec((None, tq, hps * DIFF_V), lambda b, h, i: (b, i, h)),
                  pl.BlockSpec((None, s, hps * DIFF_V), lambda b, h, i: (b, 0, h)),
                  pl.BlockSpec((None, s, hps * DIFF_V), lambda b, h, i: (b, 0, h)),
                  const((1, DIFF_HEAD_DIM)), const((1, DIFF_HEAD_DIM)),
                  const((1, DIFF_HEAD_DIM)), const((1, DIFF_HEAD_DIM)), const((1, DIFF_V))],
        out_specs=pl.BlockSpec((None, tq, hps * DIFF_V), lambda b, h, i: (b, i, h)),
        out_shape=jax.ShapeDtypeStruct((bsz, s, d), BF16),
        scratch_shapes=[pltpu.VMEM((2 * hps, tq, LANES), F32), pltpu.VMEM((2 * hps, tq, LANES), F32),
                        pltpu.VMEM((2 * hps, tq, DIFF_V), F32)],
        compiler_params=_cparams(("parallel", "parallel", "arbitrary"), 56),
        name="diff_attention",
    )(q, k, v, lam_q1.reshape(1, -1), lam_k1.reshape(1, -1), lam_q2.reshape(1, -1), lam_k2.reshape(1, -1),
      subln.reshape(1, -1))
    return out.reshape(bsz * s, d)


def _mem_attn_kernel(x_ref, g_ref, wq_ref, kv_ref, qg_ref, kg_ref, wo_ref, o_ref):
    x = x_ref[...]
    xn = _rms(x, g_ref[...]).astype(BF16)
    q = jnp.dot(xn, wq_ref[...], preferred_element_type=F32)
    kv = kv_ref[...]
    outs = []
    for h in range(MEM_HEADS):
        lo = h * MEM_HEAD_DIM
        qh = (_rms(q[:, lo:lo + MEM_HEAD_DIM], qg_ref[...]) * (MEM_HEAD_DIM ** -0.5)).astype(BF16)
        kh = _rms(kv[:, lo:lo + MEM_HEAD_DIM], kg_ref[...]).astype(BF16)
        vh = kv[:, MEM_WIDTH + lo:MEM_WIDTH + lo + MEM_HEAD_DIM].astype(BF16)
        s = _dot_nt(qh, kh)
        p = jnp.exp(s - jnp.max(s, axis=-1, keepdims=True))
        o = jnp.dot(p.astype(BF16), vh, preferred_element_type=F32)
        outs.append((o / jnp.sum(p, axis=-1, keepdims=True)).astype(BF16))
    o_ref[...] = x + jnp.dot(jnp.concatenate(outs, axis=1), wo_ref[...], preferred_element_type=F32)


def memory_attention(x, mem_kv, norm_g, w_q, q_gain, k_gain, w_o, bsz, s, tm):
    t, d = x.shape
    m = mem_kv.shape[0] // bsz
    nb = s // tm
    const = lambda shape: pl.BlockSpec(shape, lambda i: (0, 0))
    return pl.pallas_call(
        _mem_attn_kernel,
        grid=(t // tm,),
        in_specs=[pl.BlockSpec((tm, d), lambda i: (i, 0)),
                  const((1, d)), const((d, MEM_WIDTH)),
                  pl.BlockSpec((m, 2 * MEM_WIDTH), lambda i: (i // nb, 0)),
                  const((1, MEM_HEAD_DIM)), const((1, MEM_HEAD_DIM)), const((MEM_WIDTH, d))],
        out_specs=pl.BlockSpec((tm, d), lambda i: (i, 0)),
        out_shape=jax.ShapeDtypeStruct((t, d), F32),
        compiler_params=_cparams(("parallel",), 56),
        name="memory_attention",
    )(x, norm_g.reshape(1, d), w_q.astype(BF16), mem_kv, q_gain.reshape(1, -1), k_gain.reshape(1, -1),
      w_o.astype(BF16))


def _compare_exchange(lst, i, j):
    a, b = lst[i], lst[j]
    if b is None:
        return
    if a is None:
        lst[i], lst[j] = b, None
        return
    lst[i], lst[j] = jnp.maximum(a, b), jnp.minimum(a, b)


def _bitonic_sort(lst):
    n = len(lst)
    k = 2
    while k <= n:
        j = k // 2
        while j >= 1:
            for i in range(n):
                l = i ^ j
                if l > i:
                    if i & k == 0:
                        _compare_exchange(lst, i, l)
                    else:
                        _compare_exchange(lst, l, i)
            j //= 2
        k *= 2


def _bitonic_merge(lst):
    j = len(lst) // 2
    while j >= 1:
        for i in range(len(lst)):
            if i ^ j > i:
                _compare_exchange(lst, i, i ^ j)
        j //= 2


def _top_values(x, k):
    sub = 8
    nb = x.shape[0] // sub
    size = 1
    while size < nb:
        size *= 2
    lst = [x[i * sub:(i + 1) * sub, :] for i in range(nb)] + [None] * (size - nb)
    _bitonic_sort(lst)
    lst = (lst + [None] * k)[:k]
    shift = sub // 2
    while shift >= 1:
        other = [None if v is None else pltpu.roll(v, shift, 0) for v in lst]
        merged = []
        for i in range(k):
            a, b = lst[i], other[k - 1 - i]
            merged.append(b if a is None else a if b is None else jnp.maximum(a, b))
        _bitonic_merge(merged)
        lst = merged
        shift //= 2
    return [v[0:1, :] for v in lst]


_PAIRS = [(i, j) for i in range(PEER_TOPK) for j in range(PEER_TOPK) if (i + 1) * (j + 1) <= PEER_TOPK]
_PAIR_ROWS = -(-len(_PAIRS) // 8) * 8


def _partner_threshold(s0, b, tau):
    hit = lambda v: s0 + v >= tau
    m8 = hit(b[7])
    thr = jnp.where(m8, b[7], jnp.inf)
    b4 = jnp.where(m8, b[11], b[3])
    m4 = hit(b4)
    thr = jnp.where(m4, b4, thr)
    b2 = jnp.where(m8, jnp.where(m4, b[13], b[9]), jnp.where(m4, b[5], b[1]))
    m2 = hit(b2)
    thr = jnp.where(m2, b2, thr)
    b1 = jnp.where(m8, jnp.where(m4, jnp.where(m2, b[14], b[12]), jnp.where(m2, b[10], b[8])),
                   jnp.where(m4, jnp.where(m2, b[6], b[4]), jnp.where(m2, b[2], b[0])))
    thr = jnp.where(hit(b1), b1, thr)
    return jnp.where(hit(b[15]), b[15], thr)


def _peer_route_kernel(x_ref, g_ref, wq_ref, sk_ref, xn_out, thr_out, e0_out, s1_out, e1_out, sc_ref):
    tm = x_ref.shape[0]
    K = PEER_KEYS
    RB = 64
    xn32 = _rms(x_ref[...], g_ref[...])
    xn_out[...] = xn32.T.astype(BF16)
    q = jnp.dot(xn32.astype(BF16), wq_ref[...], preferred_element_type=F32).astype(BF16)
    for g in range(2 * PEER_HEADS):
        sc_ref[g * K:(g + 1) * K, :] = _dot_nt(sk_ref[g], q[:, g * PEER_HALF:(g + 1) * PEER_HALF])
    for h in range(PEER_HEADS):
        lo0, lo1 = 2 * h * K, (2 * h + 1) * K
        a = _top_values(sc_ref[lo0:lo0 + K, :], PEER_TOPK)
        b = _top_values(sc_ref[lo1:lo1 + K, :], PEER_TOPK)
        cands = [a[i] + b[j] for i, j in _PAIRS]
        cands += [jnp.full((1, tm), -jnp.inf, F32)] * (_PAIR_ROWS - len(_PAIRS))
        cand = jnp.concatenate(cands, axis=0)
        tau = _top_values(cand, PEER_TOPK)[-1]
        z = jnp.sum(jnp.where(cand >= tau, jnp.exp(cand - (a[0] + b[0])), 0.0), axis=0, keepdims=True)
        for c in range(tm // LANES):
            cs = slice(c * LANES, (c + 1) * LANES)
            for rb in range(K // RB):
                s0 = sc_ref[lo0 + rb * RB:lo0 + (rb + 1) * RB, cs]
                thr_out[h * K + rb * RB:h * K + (rb + 1) * RB, cs] = _partner_threshold(
                    s0, [v[:, cs] for v in b], tau[:, cs])
        e0_out[h * K:(h + 1) * K, :] = jnp.exp(sc_ref[lo0:lo0 + K, :] - a[0])
        s1_out[h * K:(h + 1) * K, :] = sc_ref[lo1:lo1 + K, :]
        e1_out[h * K:(h + 1) * K, :] = jnp.exp(sc_ref[lo1:lo1 + K, :] - b[0]) * (1.0 / z)


def peer_route(x, norm_g, w_q, sub_keys, tm):
    t, d = x.shape
    nq = 2 * PEER_HEADS * PEER_HALF
    nk = PEER_HEADS * PEER_KEYS
    sk = sub_keys.reshape(2 * PEER_HEADS, PEER_KEYS, PEER_HALF).astype(BF16)
    col = lambda rows: pl.BlockSpec((rows, tm), lambda i: (0, i))
    return pl.pallas_call(
        _peer_route_kernel,
        grid=(t // tm,),
        in_specs=[pl.BlockSpec((tm, d), lambda i: (i, 0)),
                  pl.BlockSpec((1, d), lambda i: (0, 0)),
                  pl.BlockSpec((d, nq), lambda i: (0, 0)),
                  pl.BlockSpec((2 * PEER_HEADS, PEER_KEYS, PEER_HALF), lambda i: (0, 0, 0))],
        out_specs=[col(d), col(nk), col(nk), col(nk), col(nk)],
        out_shape=[jax.ShapeDtypeStruct((d, t), BF16)] + [jax.ShapeDtypeStruct((nk, t), F32)] * 4,
        scratch_shapes=[pltpu.VMEM((2 * nk, tm), F32)],
        compiler_params=_cparams(("parallel",), 56),
        name="peer_route",
    )(x, norm_g.reshape(1, d), w_q.astype(BF16), sk)


def _gelu(x):
    return 0.5 * x * (1.0 + lax.erf(x * (2.0 ** -0.5)))


def _peer_gated_act(blk, half, js, nr, xu_ref, w_ref, thr_ref, e0_ref, s1_ref, e1_ref):
    K = PEER_KEYS
    tmh = xu_ref.shape[1]
    for j in js:
        r = jnp.minimum(blk * nr + j, K - 1)
        thr_rows = [thr_ref[pl.ds(h * K + r, 1), :] for h in range(PEER_HEADS)]
        e0_rows = [e0_ref[pl.ds(h * K + r, 1), :] for h in range(PEER_HEADS)]
        for c in range(tmh // LANES):
            cl = slice(c * LANES, (c + 1) * LANES)
            cs = slice(half * tmh + c * LANES, half * tmh + (c + 1) * LANES)
            gate = None
            for h in range(PEER_HEADS):
                part = jnp.where(s1_ref[h * K:(h + 1) * K, cs] >= thr_rows[h][:, cs],
                                 e1_ref[h * K:(h + 1) * K, cs] * e0_rows[h][:, cs], 0.0)
                gate = part if gate is None else gate + part
            w_ref[j * K:(j + 1) * K, cl] = (_gelu(xu_ref[j * K:(j + 1) * K, cl]) * gate).astype(BF16)


def _peer_dense_kernel(xnt_ref, thr_ref, e0_ref, s1_ref, e1_ref, u_ref, vta_ref, vtb_ref, o_ref,
                       xu0_ref, xu1_ref, w0_ref, w1_ref, *, nr):
    g = pl.program_id(1)
    last = pl.num_programs(1) - 1
    eb = nr * PEER_KEYS
    tmh = xnt_ref.shape[1] // 2
    xu, w = (xu0_ref, xu1_ref), (w0_ref, w1_ref)

    def score(s):
        hd = s % 2
        xu[hd][...] = jnp.dot(u_ref[(s // 2) * eb:(s // 2 + 1) * eb, :], xnt_ref[:, hd * tmh:(hd + 1) * tmh],
                              preferred_element_type=F32)

    def value(s):
        hd = s % 2
        vt_ref = vtb_ref if s < 2 else vta_ref
        o_ref[:, hd * tmh:(hd + 1) * tmh] += jnp.dot(vt_ref[...], w[hd][...], preferred_element_type=F32)

    def act(s, js):
        hv = (s + 1) % 2
        _peer_gated_act(2 * g + (s - 1) // 2, hv, js, nr, xu[hv], w[hv], thr_ref, e0_ref, s1_ref, e1_ref)

    def slot(s):
        score(s)
        act(s, range(0, nr // 2))
        value(s)
        act(s, range(nr // 2, nr))

    @pl.when(g == 0)
    def _():
        o_ref[...] = jnp.zeros(o_ref.shape, F32)
        score(0)
        score(1)
        act(1, range(nr))

    @pl.when(g > 0)
    def _():
        slot(0)
        slot(1)

    @pl.when(g < last)
    def _():
        slot(2)
        slot(3)


def peer_dense(xnt, thr, e0, s1, e1, u_tab, v_tab, tm, eb):
    d, t = xnt.shape
    n_exp = u_tab.shape[0]
    nblk = n_exp // eb
    nk = thr.shape[0]
    u16 = u_tab.astype(BF16)
    vt16 = v_tab.astype(BF16).T
    col = lambda rows: pl.BlockSpec((rows, tm), lambda i, g: (0, i))
    return pl.pallas_call(
        functools.partial(_peer_dense_kernel, nr=eb // PEER_KEYS),
        grid=(t // tm, nblk // 2 + 1),
        in_specs=[col(d), col(nk), col(nk), col(nk), col(nk),
                  pl.BlockSpec((2 * eb, d), lambda i, g: (jnp.minimum(g, nblk // 2 - 1), 0)),
                  pl.BlockSpec((d, eb), lambda i, g: (0, jnp.minimum(2 * g, nblk - 1))),
                  pl.BlockSpec((d, eb), lambda i, g: (0, jnp.maximum(2 * g - 1, 0)))],
        out_specs=pl.BlockSpec((d, tm), lambda i, g: (0, i)),
        out_shape=jax.ShapeDtypeStruct((d, t), F32),
        scratch_shapes=[pltpu.VMEM((eb, tm // 2), F32),
                        pltpu.VMEM((eb, tm // 2), F32),
                        pltpu.VMEM((eb, tm // 2), BF16),
                        pltpu.VMEM((eb, tm // 2), BF16)],
        compiler_params=_cparams(("parallel", "arbitrary"), 56),
        name="peer_dense",
    )(xnt, thr, e0, s1, e1, u16, vt16, vt16)


def _transpose_add_kernel(x_ref, pt_ref, o_ref):
    o_ref[...] = x_ref[...] + pt_ref[...].T


def transpose_add(x, pt, tm):
    t, d = x.shape
    return pl.pallas_call(
        _transpose_add_kernel,
        grid=(t // tm,),
        in_specs=[pl.BlockSpec((tm, d), lambda i: (i, 0)), pl.BlockSpec((d, tm), lambda i: (0, i))],
        out_specs=pl.BlockSpec((tm, d), lambda i: (i, 0)),
        out_shape=jax.ShapeDtypeStruct((t, d), F32),
        compiler_params=_cparams(("parallel",), 48),
        name="transpose_add",
    )(x, pt)


def _even_w_in(w_in):
    o = np.cumsum([0, MLA_Q_RANK, MLA_KV_RANK, MLA_ROPE, SSD_INNER, SSD_CONV_DIM, SSD_HEADS])
    c_q, c_kv, k_r, z, xbc, dt = (w_in[:, o[i]:o[i + 1]] for i in range(6))
    dt = jnp.pad(dt, ((0, 0), (0, LANES - SSD_HEADS)))
    return jnp.concatenate([xbc, z, c_q, c_kv, _pad_rope_cols(k_r), dt], axis=1).astype(BF16)


def _tile(n, pref):
    return pref if n % pref == 0 else n


def kernel(x, mem, positions, norm_mix, norm_mem, norm_mem_kv, norm_ffn, mem_w_q, mem_w_kv, mem_q_gain, mem_k_gain, mem_w_o, peer_w_q, peer_sub_keys, peer_u, peer_v, ev_w_in, ev_q_norm, ev_kv_norm, ev_w_uq, ev_w_ukv, ev_q_gain, ev_k_gain, ev_conv_w, ev_conv_b, ev_dt_bias, ev_a_log, ev_d_skip, ev_ssd_norm, ev_w_out, od_w_qkv, od_q_gain, od_k_gain, od_lam_q1, od_lam_k1, od_lam_q2, od_lam_k2, od_subln, od_w_o):
    bsz, s, d = x.shape
    t = bsz * s
    depth = norm_mix.shape[0]
    tm = _tile(s, 512)
    tq = _tile(s, 512)
    tm2 = _tile(t, 2 * tm)
    xf = x.reshape(t, d)
    pos = positions.reshape(t)
    mem_f = mem.reshape(-1, d)
    cos_mla, sin_mla = rope_tables(pos, MLA_ROPE // 2, LANES // 2)
    cos_dif, sin_dif = rope_tables(pos, DIFF_HEAD_DIM // 2, LANES // 2)

    for layer in range(depth):
        i = layer // 2
        if layer % 2 == 0:
            proj = norm_matmul(xf, norm_mix[layer], _even_w_in(ev_w_in[i]), tm, IN_TOTAL // 2)
            q, k, v = mla_prep(proj, ev_q_norm[i], ev_kv_norm[i], ev_w_uq[i], ev_w_ukv[i],
                               ev_q_gain[i], ev_k_gain[i], cos_mla, sin_mla, tm)
            mla_out = mla_attention(q, k, v, bsz, s, tq)
            y = ssd_mixer(proj, ev_conv_w[i], ev_conv_b[i], ev_dt_bias[i], ev_a_log[i], ev_d_skip[i],
                          ev_ssd_norm[i], bsz, s)
            w_out = ev_w_out[i].astype(BF16)
            n_mla = MLA_HEADS * MLA_V
            xf = matmul_residual([mla_out, y], [w_out[:n_mla], w_out[n_mla:]], xf, tm2, 1024)
        else:
            qkv = norm_matmul(xf, norm_mix[layer], od_w_qkv[i].astype(BF16), tm2, 1024)
            q, k, v = diff_prep(qkv, od_q_gain[i], od_k_gain[i], cos_dif, sin_dif, tm)
            lam_init = 0.8 - 0.6 * math.exp(-0.3 * layer)
            attn = diff_attention(q, k, v, od_lam_q1[i], od_lam_k1[i], od_lam_q2[i], od_lam_k2[i],
                                  od_subln[i], lam_init, bsz, s, tq)
            xf = matmul_residual([attn], [od_w_o[i].astype(BF16)], xf, tm2, 1024)
        mem_kv = norm_matmul(mem_f, norm_mem_kv[layer], mem_w_kv[layer].astype(BF16),
                             _tile(mem_f.shape[0], 512), 2 * MEM_WIDTH)
        xf = memory_attention(xf, mem_kv, norm_mem[layer], mem_w_q[layer], mem_q_gain[layer],
                              mem_k_gain[layer], mem_w_o[layer], bsz, s, tm)
        routing = peer_route(xf, norm_ffn[layer], peer_w_q[layer], peer_sub_keys[layer], tm)
        xf = transpose_add(xf, peer_dense(*routing, peer_u[layer], peer_v[layer], tm, 512), tm)
    return xf.reshape(bsz, s, d)
```

```python
import functools
import math

import jax
import jax.numpy as jnp
import numpy as np
from jax import lax
from jax.experimental import pallas as pl
from jax.experimental.pallas import tpu as pltpu

F32 = jnp.float32
BF16 = jnp.bfloat16

EPS = 1e-6
ROPE_THETA = 10000.0
LANES = 128

MLA_HEADS = 8
MLA_Q_RANK = 512
MLA_KV_RANK = 512
MLA_NOPE = 128
MLA_ROPE = 64
MLA_V = 128
MLA_QK = MLA_NOPE + MLA_ROPE
MLA_QK_PAD = 256

SSD_HEADS = 16
SSD_HEAD_DIM = 64
SSD_INNER = SSD_HEADS * SSD_HEAD_DIM
SSD_GROUPS = 4
SSD_STATE = 128
SSD_CONV = 4
SSD_CHUNK = 128
SSD_CONV_DIM = SSD_INNER + 2 * SSD_GROUPS * SSD_STATE
SSD_HPG = SSD_HEADS // SSD_GROUPS

DIFF_HEADS = 8
DIFF_HEAD_DIM = 128
DIFF_V = 256

MEM_HEADS = 4
MEM_HEAD_DIM = 128
MEM_WIDTH = MEM_HEADS * MEM_HEAD_DIM

PEER_HEADS = 8
PEER_KEYS = 128
PEER_TOPK = 16
PEER_HALF = 128

IN_XBC = 0
IN_Z = IN_XBC + SSD_CONV_DIM
IN_CQ = IN_Z + SSD_INNER
IN_CKV = IN_CQ + MLA_Q_RANK
IN_KR = IN_CKV + MLA_KV_RANK
IN_DT = IN_KR + LANES
IN_TOTAL = IN_DT + LANES

NEG_BIG = -1e30


def _cparams(sem, vmem_mb):
    return pltpu.CompilerParams(dimension_semantics=sem, vmem_limit_bytes=vmem_mb * 1024 * 1024)


def _rms(x, g, n=None):
    ss = jnp.sum(x * x, axis=-1, keepdims=True)
    n = x.shape[-1] if n is None else n
    return x * lax.rsqrt(ss * (1.0 / n) + EPS) * g


def _dot_nt(a, b):
    return lax.dot_general(a, b, (((1,), (1,)), ((), ())), preferred_element_type=F32)


def _norm_mm_kernel(x_ref, g_ref, w_ref, o_ref, xn_ref):
    @pl.when(pl.program_id(1) == 0)
    def _():
        xn_ref[...] = _rms(x_ref[...], g_ref[...]).astype(BF16)

    o_ref[...] = jnp.dot(xn_ref[...], w_ref[...], preferred_element_type=F32).astype(o_ref.dtype)


def norm_matmul(x, g, w, tm, tn, out_dtype=F32):
    t, d = x.shape
    n = w.shape[1]
    return pl.pallas_call(
        _norm_mm_kernel,
        grid=(t // tm, n // tn),
        in_specs=[pl.BlockSpec((tm, d), lambda i, j: (i, 0)),
                  pl.BlockSpec((1, d), lambda i, j: (0, 0)),
                  pl.BlockSpec((d, tn), lambda i, j: (0, j))],
        out_specs=pl.BlockSpec((tm, tn), lambda i, j: (i, j)),
        out_shape=jax.ShapeDtypeStruct((t, n), out_dtype),
        scratch_shapes=[pltpu.VMEM((tm, d), BF16)],
        compiler_params=_cparams(("parallel", "arbitrary"), 56),
        name="norm_matmul",
    )(x, g.reshape(1, d), w)


def _mm_res_kernel(*refs, n_a):
    res_ref, o_ref = refs[2 * n_a], refs[2 * n_a + 1]
    acc = res_ref[...]
    for a_ref, w_ref in zip(refs[:n_a], refs[n_a:2 * n_a]):
        acc = acc + jnp.dot(a_ref[...], w_ref[...], preferred_element_type=F32)
    o_ref[...] = acc


def matmul_residual(a_list, w_list, res, tm, tn):
    t, n = res.shape
    n_a = len(a_list)
    in_specs = ([pl.BlockSpec((tm, a.shape[1]), lambda i, j: (i, 0)) for a in a_list]
                + [pl.BlockSpec((w.shape[0], tn), lambda i, j: (0, j)) for w in w_list]
                + [pl.BlockSpec((tm, tn), lambda i, j: (i, j))])
    return pl.pallas_call(
        functools.partial(_mm_res_kernel, n_a=n_a),
        grid=(t // tm, n // tn),
        in_specs=in_specs,
        out_specs=pl.BlockSpec((tm, tn), lambda i, j: (i, j)),
        out_shape=jax.ShapeDtypeStruct((t, n), F32),
        compiler_params=_cparams(("parallel", "parallel"), 56),
        name="matmul_residual",
    )(*a_list, *w_list, res)


def _cast_kernel(x_ref, o_ref):
    o_ref[...] = x_ref[...].astype(o_ref.dtype)


def _cast_t_kernel(x_ref, o_ref):
    o_ref[...] = x_ref[...].T.astype(o_ref.dtype)


def cast_bf16(x, tr):
    r, c = x.shape
    return pl.pallas_call(
        _cast_kernel,
        grid=(r // tr,),
        in_specs=[pl.BlockSpec((tr, c), lambda i: (i, 0))],
        out_specs=pl.BlockSpec((tr, c), lambda i: (i, 0)),
        out_shape=jax.ShapeDtypeStruct((r, c), BF16),
        compiler_params=_cparams(("parallel",), 32),
        name="cast_bf16",
    )(x)


def cast_transpose_bf16(x, tr):
    r, c = x.shape
    return pl.pallas_call(
        _cast_t_kernel,
        grid=(r // tr,),
        in_specs=[pl.BlockSpec((tr, c), lambda i: (i, 0))],
        out_specs=pl.BlockSpec((c, tr), lambda i: (0, i)),
        out_shape=jax.ShapeDtypeStruct((c, r), BF16),
        compiler_params=_cparams(("parallel",), 32),
        name="cast_transpose_bf16",
    )(x)


def _rope_tab_kernel(pos_ref, inv_ref, sgn_ref, cos_ref, sin_ref):
    ang = pos_ref[...].astype(F32) * inv_ref[...]
    cos_ref[...] = jnp.cos(ang)
    sin_ref[...] = jnp.sin(ang) * sgn_ref[...]


def rope_tables(pos, half, stride):
    t = pos.shape[0]
    lane = np.arange(LANES)
    j = lane % stride
    inv = np.where(j < half, 1.0 / (ROPE_THETA ** (np.minimum(j, half - 1).astype(np.float32) / half)), 0.0)
    sgn = np.where(lane < stride, -1.0, 1.0)
    tm = min(t, 1024)
    return pl.pallas_call(
        _rope_tab_kernel,
        grid=(t // tm,),
        in_specs=[pl.BlockSpec((tm, 1), lambda i: (i, 0)),
                  pl.BlockSpec((1, LANES), lambda i: (0, 0)),
                  pl.BlockSpec((1, LANES), lambda i: (0, 0))],
        out_specs=[pl.BlockSpec((tm, LANES), lambda i: (i, 0))] * 2,
        out_shape=[jax.ShapeDtypeStruct((t, LANES), F32)] * 2,
        compiler_params=_cparams(("parallel",), 32),
        name="rope_tables",
    )(pos.reshape(t, 1), jnp.asarray(inv, F32).reshape(1, LANES), jnp.asarray(sgn, F32).reshape(1, LANES))


def _rope_seg(x, cos, sin):
    return x * cos + pltpu.roll(x, LANES // 2, 1) * sin


def _rep(x, n):
    return x if n == 1 else jnp.concatenate([x] * n, axis=1)


def _flash(chains, n_full, tq, tk):
    for (_, _, _, _, _, _, _, m_ref, l_ref, acc_ref) in chains:
        m_ref[...] = jnp.full(m_ref.shape, NEG_BIG, F32)
        l_ref[...] = jnp.zeros(l_ref.shape, F32)
        acc_ref[...] = jnp.zeros(acc_ref.shape, F32)

    def step(j, masked):
        start = pl.multiple_of(j * tk, tk)
        if masked:
            row = lax.broadcasted_iota(jnp.int32, (tq, tk), 0)
            col = lax.broadcasted_iota(jnp.int32, (tq, tk), 1)
            keep = col <= row
        for (q, k_ref, koff, dk, v_ref, voff, dv, m_ref, l_ref, acc_ref) in chains:
            s = _dot_nt(q, k_ref[pl.ds(start, tk), koff:koff + dk])
            if masked:
                s = jnp.where(keep, s, NEG_BIG)
            m_prev = m_ref[...]
            m_new = jnp.maximum(m_prev, jnp.max(s, axis=-1, keepdims=True))
            p = jnp.exp(s - _rep(m_new, tk // LANES))
            alpha = jnp.exp(m_prev - m_new)
            l_ref[...] = alpha * l_ref[...] + jnp.sum(p, axis=-1, keepdims=True)
            acc_ref[...] = (_rep(alpha, dv // LANES) * acc_ref[...]
                            + jnp.dot(p.astype(BF16), v_ref[pl.ds(start, tk), voff:voff + dv],
                                      preferred_element_type=F32))
            m_ref[...] = m_new

    def body(j, carry):
        step(j, False)
        return carry

    lax.fori_loop(0, n_full, body, 0)
    step(n_full, True)
    return [acc_ref[...] / _rep(l_ref[...], dv // LANES) for (_, _, _, _, _, _, dv, _, l_ref, acc_ref) in chains]


def _mla_prep_kernel(cq_ref, ckv_ref, kr_ref, qn_ref, kvn_ref, wuq_ref, wukv_ref, qg_ref, kg_ref,
                     cos_ref, sin_ref, q_out, k_out, v_out):
    cq = _rms(cq_ref[...], qn_ref[...]).astype(BF16)
    qf = jnp.dot(cq, wuq_ref[...], preferred_element_type=F32)
    ckv = _rms(ckv_ref[...], kvn_ref[...]).astype(BF16)
    kvf = jnp.dot(ckv, wukv_ref[...], preferred_element_type=F32)
    kr = kr_ref[...]
    kr_ss = jnp.sum(kr * kr, axis=-1, keepdims=True)
    cos, sin = cos_ref[...], sin_ref[...]
    qg, kg = qg_ref[...], kg_ref[...]
    scale = MLA_QK ** -0.5
    for h in range(MLA_HEADS):
        lo = h * MLA_QK_PAD
        qh = _rms(qf[:, lo:lo + MLA_QK_PAD], qg, MLA_QK) * scale
        q_out[:, lo:lo + LANES] = qh[:, :LANES].astype(BF16)
        q_out[:, lo + LANES:lo + 2 * LANES] = _rope_seg(qh[:, LANES:], cos, sin).astype(BF16)
        kn = kvf[:, lo:lo + MLA_NOPE]
        r = lax.rsqrt((jnp.sum(kn * kn, axis=-1, keepdims=True) + kr_ss) * (1.0 / MLA_QK) + EPS)
        k_out[:, lo:lo + LANES] = (kn * r * kg[:, :LANES]).astype(BF16)
        k_out[:, lo + LANES:lo + 2 * LANES] = _rope_seg(kr * r * kg[:, LANES:], cos, sin).astype(BF16)
        v_out[:, h * MLA_V:(h + 1) * MLA_V] = kvf[:, lo + MLA_NOPE:lo + MLA_QK_PAD].astype(BF16)


def _pad_rope_cols(w):
    half = MLA_ROPE // 2
    z = jnp.zeros(w.shape[:-1] + (half,), w.dtype)
    return jnp.concatenate([w[..., :half], z, w[..., half:], z], axis=-1)


def mla_prep(proj, q_norm, kv_norm, w_uq, w_ukv, q_gain, k_gain, cos, sin, tm):
    t = proj.shape[0]
    hq = MLA_HEADS * MLA_QK_PAD
    wq = w_uq.reshape(MLA_Q_RANK, MLA_HEADS, MLA_QK)
    wq = jnp.concatenate([wq[..., :MLA_NOPE], _pad_rope_cols(wq[..., MLA_NOPE:])], axis=-1)
    wq = wq.reshape(MLA_Q_RANK, hq).astype(BF16)
    qg = jnp.concatenate([q_gain[:MLA_NOPE], _pad_rope_cols(q_gain[MLA_NOPE:])]).reshape(1, MLA_QK_PAD)
    kg = jnp.concatenate([k_gain[:MLA_NOPE], _pad_rope_cols(k_gain[MLA_NOPE:])]).reshape(1, MLA_QK_PAD)
    const = lambda shape: pl.BlockSpec(shape, lambda i: (0, 0))
    return pl.pallas_call(
        _mla_prep_kernel,
        grid=(t // tm,),
        in_specs=[pl.BlockSpec((tm, MLA_Q_RANK), lambda i: (i, IN_CQ // MLA_Q_RANK)),
                  pl.BlockSpec((tm, MLA_KV_RANK), lambda i: (i, IN_CKV // MLA_KV_RANK)),
                  pl.BlockSpec((tm, LANES), lambda i: (i, IN_KR // LANES)),
                  const((1, MLA_Q_RANK)), const((1, MLA_KV_RANK)),
                  const((MLA_Q_RANK, hq)), const((MLA_KV_RANK, hq)),
                  const((1, MLA_QK_PAD)), const((1, MLA_QK_PAD)),
                  pl.BlockSpec((tm, LANES), lambda i: (i, 0)),
                  pl.BlockSpec((tm, LANES), lambda i: (i, 0))],
        out_specs=[pl.BlockSpec((tm, hq), lambda i: (i, 0)),
                   pl.BlockSpec((tm, hq), lambda i: (i, 0)),
                   pl.BlockSpec((tm, MLA_HEADS * MLA_V), lambda i: (i, 0))],
        out_shape=[jax.ShapeDtypeStruct((t, hq), BF16), jax.ShapeDtypeStruct((t, hq), BF16),
                   jax.ShapeDtypeStruct((t, MLA_HEADS * MLA_V), BF16)],
        compiler_params=_cparams(("parallel",), 56),
        name="mla_prep",
    )(proj, proj, proj, q_norm.reshape(1, -1), kv_norm.reshape(1, -1), wq, w_ukv.astype(BF16), qg, kg, cos, sin)


MLA_HPS = 4
DIFF_HPS = 2


def _mla_attn_kernel(q_ref, k_ref, v_ref, o_ref, m_ref, l_ref, acc_ref, *, tq):
    chains = [(q_ref[:, h * MLA_QK_PAD:(h + 1) * MLA_QK_PAD], k_ref, h * MLA_QK_PAD, MLA_QK_PAD,
               v_ref, h * MLA_V, MLA_V, m_ref.at[h], l_ref.at[h], acc_ref.at[h]) for h in range(MLA_HPS)]
    outs = _flash(chains, pl.program_id(2), tq, tq)
    for h in range(MLA_HPS):
        o_ref[:, h * MLA_V:(h + 1) * MLA_V] = outs[h].astype(o_ref.dtype)


def mla_attention(q, k, v, bsz, s, tq):
    hq = MLA_HEADS * MLA_QK_PAD
    q, k = q.reshape(bsz, s, hq), k.reshape(bsz, s, hq)
    v = v.reshape(bsz, s, MLA_HEADS * MLA_V)
    hps = MLA_HPS
    out = pl.pallas_call(
        functools.partial(_mla_attn_kernel, tq=tq),
        grid=(bsz, MLA_HEADS // hps, s // tq),
        in_specs=[pl.BlockSpec((None, tq, hps * MLA_QK_PAD), lambda b, h, i: (b, i, h)),
                  pl.BlockSpec((None, s, hps * MLA_QK_PAD), lambda b, h, i: (b, 0, h)),
                  pl.BlockSpec((None, s, hps * MLA_V), lambda b, h, i: (b, 0, h))],
        out_specs=pl.BlockSpec((None, tq, hps * MLA_V), lambda b, h, i: (b, i, h)),
        out_shape=jax.ShapeDtypeStruct((bsz, s, MLA_HEADS * MLA_V), BF16),
        scratch_shapes=[pltpu.VMEM((hps, tq, LANES), F32), pltpu.VMEM((hps, tq, LANES), F32),
                        pltpu.VMEM((hps, tq, MLA_V), F32)],
        compiler_params=_cparams(("parallel", "parallel", "arbitrary"), 48),
        name="mla_attention",
    )(q, k, v)
    return out.reshape(bsz * s, MLA_HEADS * MLA_V)


def _silu(x):
    return x * (1.0 / (1.0 + jnp.exp(-x)))


def _softplus(x):
    return jnp.maximum(x, 0.0) + jnp.log(1.0 + jnp.exp(-jnp.abs(x)))


def _ssd_kernel(xbc_ref, z_ref, dt_ref, cw_ref, cb_ref, dtb_ref, alog_ref, dsk_ref, nrm_ref, y_ref,
                xc_ref, st_ref):
    c = pl.program_id(1)
    L, P, N, G, HPG = SSD_CHUNK, SSD_HEAD_DIM, SSD_STATE, SSD_GROUPS, SSD_HPG
    TAIL = 8

    @pl.when(c == 0)
    def _():
        xc_ref[0:TAIL, :] = jnp.zeros((TAIL, SSD_CONV_DIM), F32)
        st_ref[...] = jnp.zeros(st_ref.shape, F32)

    xc_ref[TAIL:TAIL + L, :] = xbc_ref[...]
    conv = cb_ref[...] + jnp.zeros((L, SSD_CONV_DIM), F32)
    for k in range(SSD_CONV):
        off = TAIL - (SSD_CONV - 1) + k
        conv = conv + xc_ref[off:off + L, :] * cw_ref[k:k + 1, :]
    xc_ref[0:TAIL, :] = xc_ref[L:L + TAIL, :]
    xbc = _silu(conv)

    dt = _softplus(dt_ref[...] + dtb_ref[...])
    adt = dt * (-jnp.exp(alog_ref[...]))
    li = lax.broadcasted_iota(jnp.int32, (L, L), 0)
    si = lax.broadcasted_iota(jnp.int32, (L, L), 1)
    tri = (si <= li).astype(F32)
    a_cs = jnp.dot(tri, adt, preferred_element_type=F32, precision=lax.Precision.HIGHEST)
    a_cs_t = a_cs.T
    a_last = a_cs[L - 1:L, :]
    dec_state = jnp.exp(a_last - a_cs)
    dec_out = jnp.exp(a_cs)
    dec_chunk = jnp.exp(a_last)
    causal = si <= li

    zs = _silu(z_ref[...])
    for g in range(G):
        bg = xbc[:, SSD_INNER + g * N:SSD_INNER + (g + 1) * N]
        cg = xbc[:, SSD_INNER + G * N + g * N:SSD_INNER + G * N + (g + 1) * N]
        bg16, cg16 = bg.astype(BF16), cg.astype(BF16)
        cb = _dot_nt(cg16, bg16)
        bgt16 = bg.T.astype(BF16)
        xs_g = xbc[:, g * HPG * P:(g + 1) * HPG * P]
        xdd, ydiag = [], []
        for hh in range(HPG):
            h = g * HPG + hh
            xs = xs_g[:, hh * P:(hh + 1) * P]
            xd = xs * dt[:, h:h + 1]
            seg = a_cs[:, h:h + 1] - a_cs_t[h:h + 1, :]
            lm = cb * jnp.exp(jnp.where(causal, seg, NEG_BIG))
            ydiag.append(jnp.dot(lm.astype(BF16), xd.astype(BF16), preferred_element_type=F32))
            xdd.append((xd * dec_state[:, h:h + 1]).astype(BF16))
        prev = st_ref[g]
        yoff = jnp.dot(cg16, prev.astype(BF16), preferred_element_type=F32)
        new = jnp.dot(bgt16, jnp.concatenate(xdd, axis=1), preferred_element_type=F32)
        ys, decs = [], []
        for hh in range(HPG):
            h = g * HPG + hh
            xs = xs_g[:, hh * P:(hh + 1) * P]
            y = ydiag[hh] + yoff[:, hh * P:(hh + 1) * P] * dec_out[:, h:h + 1] + dsk_ref[:, h:h + 1] * xs
            ys.append(y)
            decs.append(jnp.broadcast_to(dec_chunk[:, h:h + 1], (N, P)))
        st_ref[g] = prev * jnp.concatenate(decs, axis=1) + new
        yg = jnp.concatenate(ys, axis=1) * zs[:, g * HPG * P:(g + 1) * HPG * P]
        y_ref[:, g * HPG * P:(g + 1) * HPG * P] = _rms(yg, nrm_ref[:, g * HPG * P:(g + 1) * HPG * P]).astype(y_ref.dtype)


def _pad_lanes(v):
    return jnp.pad(v.astype(F32), (0, LANES - v.shape[0])).reshape(1, LANES)


def ssd_mixer(proj, conv_w, conv_b, dt_bias, a_log, d_skip, ssd_norm, bsz, s):
    nc = s // SSD_CHUNK
    const = lambda shape: pl.BlockSpec(shape, lambda b, c: (0, 0))
    return pl.pallas_call(
        _ssd_kernel,
        grid=(bsz, nc),
        in_specs=[pl.BlockSpec((SSD_CHUNK, SSD_CONV_DIM), lambda b, c: (b * nc + c, IN_XBC // SSD_CONV_DIM)),
                  pl.BlockSpec((SSD_CHUNK, SSD_INNER), lambda b, c: (b * nc + c, IN_Z // SSD_INNER)),
                  pl.BlockSpec((SSD_CHUNK, LANES), lambda b, c: (b * nc + c, IN_DT // LANES)),
                  const((SSD_CONV, SSD_CONV_DIM)), const((1, SSD_CONV_DIM)),
                  const((1, LANES)), const((1, LANES)), const((1, LANES)), const((1, SSD_INNER))],
        out_specs=pl.BlockSpec((SSD_CHUNK, SSD_INNER), lambda b, c: (b * nc + c, 0)),
        out_shape=jax.ShapeDtypeStruct((bsz * s, SSD_INNER), BF16),
        scratch_shapes=[pltpu.VMEM((SSD_CHUNK + 8, SSD_CONV_DIM), F32),
                        pltpu.VMEM((SSD_GROUPS, SSD_STATE, SSD_HPG * SSD_HEAD_DIM), F32)],
        compiler_params=_cparams(("parallel", "arbitrary"), 48),
        name="ssd_mixer",
    )(proj, proj, proj, conv_w.astype(F32), conv_b.reshape(1, -1).astype(F32),
      _pad_lanes(dt_bias), _pad_lanes(a_log), _pad_lanes(d_skip), ssd_norm.reshape(1, -1).astype(F32))


def _diff_qkv_kernel(x_ref, g_ref, w_ref, qg_ref, kg_ref, cos_ref, sin_ref, q_out, k_out, v_out, xn_ref):
    j = pl.program_id(1)

    @pl.when(j == 0)
    def _():
        xn_ref[...] = _rms(x_ref[...], g_ref[...]).astype(BF16)

    acc = jnp.dot(xn_ref[...], w_ref[...], preferred_element_type=F32)

    def normed(gain_ref, scale, out):
        cos, sin = cos_ref[...], sin_ref[...]
        for m in range(2 * DIFF_HEADS):
            lo = m * DIFF_HEAD_DIM
            y = _rms(acc[:, lo:lo + DIFF_HEAD_DIM], gain_ref[...]) * scale
            out[:, lo:lo + DIFF_HEAD_DIM] = _rope_seg(y, cos, sin).astype(BF16)

    @pl.when(j == 0)
    def _():
        normed(qg_ref, DIFF_HEAD_DIM ** -0.5, q_out)

    @pl.when(j == 1)
    def _():
        normed(kg_ref, 1.0, k_out)

    @pl.when(j == 2)
    def _():
        v_out[...] = acc.astype(BF16)


def diff_qkv(x, norm_g, w_qkv, q_gain, k_gain, cos, sin, tm):
    t, dm = x.shape
    d = 2 * DIFF_HEADS * DIFF_HEAD_DIM
    const = lambda shape: pl.BlockSpec(shape, lambda i, j: (0, 0))
    return pl.pallas_call(
        _diff_qkv_kernel,
        grid=(t // tm, 3),
        in_specs=[pl.BlockSpec((tm, dm), lambda i, j: (i, 0)),
                  const((1, dm)),
                  pl.BlockSpec((dm, d), lambda i, j: (0, j)),
                  const((1, DIFF_HEAD_DIM)), const((1, DIFF_HEAD_DIM)),
                  pl.BlockSpec((tm, LANES), lambda i, j: (i, 0)),
                  pl.BlockSpec((tm, LANES), lambda i, j: (i, 0))],
        out_specs=[pl.BlockSpec((tm, d), lambda i, j: (i, 0))] * 3,
        out_shape=[jax.ShapeDtypeStruct((t, d), BF16)] * 3,
        scratch_shapes=[pltpu.VMEM((tm, dm), BF16)],
        compiler_params=_cparams(("parallel", "arbitrary"), 56),
        name="diff_qkv",
    )(x, norm_g.reshape(1, dm), w_qkv, q_gain.reshape(1, -1), k_gain.reshape(1, -1), cos, sin)


def _diff_attn_kernel(q_ref, k_ref, v_ref, lq1_ref, lk1_ref, lq2_ref, lk2_ref, sub_ref, o_ref,
                      m_ref, l_ref, acc_ref, *, tq, lam_init):
    dh = DIFF_HEAD_DIM
    chains = [(q_ref[:, c * dh:(c + 1) * dh], k_ref, c * dh, dh, v_ref, (c // 2) * DIFF_V, DIFF_V,
               m_ref.at[c], l_ref.at[c], acc_ref.at[c]) for c in range(2 * DIFF_HPS)]
    outs = _flash(chains, pl.program_id(2), tq, tq)
    lam = (jnp.exp(jnp.sum(lq1_ref[...] * lk1_ref[...], axis=-1, keepdims=True))
           - jnp.exp(jnp.sum(lq2_ref[...] * lk2_ref[...], axis=-1, keepdims=True)) + lam_init)
    for h in range(DIFF_HPS):
        attn = outs[2 * h] - lam * outs[2 * h + 1]
        o_ref[:, h * DIFF_V:(h + 1) * DIFF_V] = (_rms(attn, sub_ref[...]) * (1.0 - lam_init)).astype(o_ref.dtype)


def diff_attention(q, k, v, lam_q1, lam_k1, lam_q2, lam_k2, subln, lam_init, bsz, s, tq):
    d = 2 * DIFF_HEADS * DIFF_HEAD_DIM
    q, k, v = q.reshape(bsz, s, d), k.reshape(bsz, s, d), v.reshape(bsz, s, d)
    hps = DIFF_HPS
    const = lambda shape: pl.BlockSpec(shape, lambda b, h, i: (0, 0))
    out = pl.pallas_call(
        functools.partial(_diff_attn_kernel, tq=tq, lam_init=lam_init),
        grid=(bsz, DIFF_HEADS // hps, s // tq),
        in_specs=[pl.BlockSpec((None, tq, hps * DIFF_V), lambda b, h, i: (b, i, h)),
                  pl.BlockSpec((None, s, hps * DIFF_V), lambda b, h, i: (b, 0, h)),
                  pl.BlockSpec((None, s, hps * DIFF_V), lambda b, h, i: (b, 0, h)),
                  const((1, DIFF_HEAD_DIM)), const((1, DIFF_HEAD_DIM)),
                  const((1, DIFF_HEAD_DIM)), const((1, DIFF_HEAD_DIM)), const((1, DIFF_V))],
        out_specs=pl.BlockSpec((None, tq, hps * DIFF_V), lambda b, h, i: (b, i, h)),
        out_shape=jax.ShapeDtypeStruct((bsz, s, d), BF16),
        scratch_shapes=[pltpu.VMEM((2 * hps, tq, LANES), F32), pltpu.VMEM((2 * hps, tq, LANES), F32),
                        pltpu.VMEM((2 * hps, tq, DIFF_V), F32)],
        compiler_params=_cparams(("parallel", "parallel", "arbitrary"), 56),
        name="diff_attention",
    )(q, k, v, lam_q1.reshape(1, -1), lam_k1.reshape(1, -1), lam_q2.reshape(1, -1), lam_k2.reshape(1, -1),
      subln.reshape(1, -1))
    return out.reshape(bsz * s, d)


def _mem_attn_kernel(x_ref, g_ref, wq_ref, kv_ref, qg_ref, kg_ref, wo_ref, o_ref):
    x = x_ref[...]
    xn = _rms(x, g_ref[...]).astype(BF16)
    q = jnp.dot(xn, wq_ref[...], preferred_element_type=F32)
    kv = kv_ref[...]
    outs = []
    for h in range(MEM_HEADS):
        lo = h * MEM_HEAD_DIM
        qh = (_rms(q[:, lo:lo + MEM_HEAD_DIM], qg_ref[...]) * (MEM_HEAD_DIM ** -0.5)).astype(BF16)
        kh = _rms(kv[:, lo:lo + MEM_HEAD_DIM], kg_ref[...]).astype(BF16)
        vh = kv[:, MEM_WIDTH + lo:MEM_WIDTH + lo + MEM_HEAD_DIM].astype(BF16)
        s = _dot_nt(qh, kh)
        p = jnp.exp(s - jnp.max(s, axis=-1, keepdims=True))
        o = jnp.dot(p.astype(BF16), vh, preferred_element_type=F32)
        outs.append((o / jnp.sum(p, axis=-1, keepdims=True)).astype(BF16))
    o_ref[...] = x + jnp.dot(jnp.concatenate(outs, axis=1), wo_ref[...], preferred_element_type=F32)


def memory_attention(x, mem_kv, norm_g, w_q, q_gain, k_gain, w_o, bsz, s, tm):
    t, d = x.shape
    m = mem_kv.shape[0] // bsz
    nb = s // tm
    const = lambda shape: pl.BlockSpec(shape, lambda i: (0, 0))
    return pl.pallas_call(
        _mem_attn_kernel,
        grid=(t // tm,),
        in_specs=[pl.BlockSpec((tm, d), lambda i: (i, 0)),
                  const((1, d)), const((d, MEM_WIDTH)),
                  pl.BlockSpec((m, 2 * MEM_WIDTH), lambda i: (i // nb, 0)),
                  const((1, MEM_HEAD_DIM)), const((1, MEM_HEAD_DIM)), const((MEM_WIDTH, d))],
        out_specs=pl.BlockSpec((tm, d), lambda i: (i, 0)),
        out_shape=jax.ShapeDtypeStruct((t, d), F32),
        compiler_params=_cparams(("parallel",), 56),
        name="memory_attention",
    )(x, norm_g.reshape(1, d), w_q.astype(BF16), mem_kv, q_gain.reshape(1, -1), k_gain.reshape(1, -1),
      w_o.astype(BF16))


def _compare_exchange(lst, i, j):
    a, b = lst[i], lst[j]
    if b is None:
        return
    if a is None:
        lst[i], lst[j] = b, None
        return
    lst[i], lst[j] = jnp.maximum(a, b), jnp.minimum(a, b)


def _bitonic_sort(lst):
    n = len(lst)
    k = 2
    while k <= n:
        j = k // 2
        while j >= 1:
            for i in range(n):
                l = i ^ j
                if l > i:
                    if i & k == 0:
                        _compare_exchange(lst, i, l)
                    else:
                        _compare_exchange(lst, l, i)
            j //= 2
        k *= 2


def _bitonic_merge(lst):
    j = len(lst) // 2
    while j >= 1:
        for i in range(len(lst)):
            if i ^ j > i:
                _compare_exchange(lst, i, i ^ j)
        j //= 2


def _top_values(x, k):
    sub = 8
    nb = x.shape[0] // sub
    size = 1
    while size < nb:
        size *= 2
    lst = [x[i * sub:(i + 1) * sub, :] for i in range(nb)] + [None] * (size - nb)
    _bitonic_sort(lst)
    lst = (lst + [None] * k)[:k]
    shift = sub // 2
    while shift >= 1:
        other = [None if v is None else pltpu.roll(v, shift, 0) for v in lst]
        merged = []
        for i in range(k):
            a, b = lst[i], other[k - 1 - i]
            merged.append(b if a is None else a if b is None else jnp.maximum(a, b))
        _bitonic_merge(merged)
        lst = merged
        shift //= 2
    return [v[0:1, :] for v in lst]


_PAIRS = [(i, j) for i in range(PEER_TOPK) for j in range(PEER_TOPK) if (i + 1) * (j + 1) <= PEER_TOPK]
_PAIR_ROWS = -(-len(_PAIRS) // 8) * 8


def _partner_threshold(s0, b, tau):
    hit = lambda v: s0 + v >= tau
    m8 = hit(b[7])
    thr = jnp.where(m8, b[7], jnp.inf)
    b4 = jnp.where(m8, b[11], b[3])
    m4 = hit(b4)
    thr = jnp.where(m4, b4, thr)
    b2 = jnp.where(m8, jnp.where(m4, b[13], b[9]), jnp.where(m4, b[5], b[1]))
    m2 = hit(b2)
    thr = jnp.where(m2, b2, thr)
    b1 = jnp.where(m8, jnp.where(m4, jnp.where(m2, b[14], b[12]), jnp.where(m2, b[10], b[8])),
                   jnp.where(m4, jnp.where(m2, b[6], b[4]), jnp.where(m2, b[2], b[0])))
    thr = jnp.where(hit(b1), b1, thr)
    return jnp.where(hit(b[15]), b[15], thr)


def _peer_route_kernel(x_ref, g_ref, wq_ref, sk_ref, xn_out, thr_out, e0_out, s1_out, e1_out, sc_ref):
    tm = x_ref.shape[0]
    K = PEER_KEYS
    RB = 64
    xn32 = _rms(x_ref[...], g_ref[...])
    xn_out[...] = xn32.T.astype(BF16)
    q = jnp.dot(xn32.astype(BF16), wq_ref[...], preferred_element_type=F32).astype(BF16)
    for g in range(2 * PEER_HEADS):
        sc_ref[g * K:(g + 1) * K, :] = _dot_nt(sk_ref[g], q[:, g * PEER_HALF:(g + 1) * PEER_HALF])
    for h in range(PEER_HEADS):
        lo0, lo1 = 2 * h * K, (2 * h + 1) * K
        a = _top_values(sc_ref[lo0:lo0 + K, :], PEER_TOPK)
        b = _top_values(sc_ref[lo1:lo1 + K, :], PEER_TOPK)
        cands = [a[i] + b[j] for i, j in _PAIRS]
        cands += [jnp.full((1, tm), -jnp.inf, F32)] * (_PAIR_ROWS - len(_PAIRS))
        cand = jnp.concatenate(cands, axis=0)
        tau = _top_values(cand, PEER_TOPK)[-1]
        z = jnp.sum(jnp.where(cand >= tau, jnp.exp(cand - (a[0] + b[0])), 0.0), axis=0, keepdims=True)
        for c in range(tm // LANES):
            cs = slice(c * LANES, (c + 1) * LANES)
            for rb in range(K // RB):
                s0 = sc_ref[lo0 + rb * RB:lo0 + (rb + 1) * RB, cs]
                thr_out[h * K + rb * RB:h * K + (rb + 1) * RB, cs] = _partner_threshold(
                    s0, [v[:, cs] for v in b], tau[:, cs])
        e0_out[h * K:(h + 1) * K, :] = jnp.exp(sc_ref[lo0:lo0 + K, :] - a[0])
        s1_out[h * K:(h + 1) * K, :] = sc_ref[lo1:lo1 + K, :]
        e1_out[h * K:(h + 1) * K, :] = jnp.exp(sc_ref[lo1:lo1 + K, :] - b[0]) * (1.0 / z)


def peer_route(x, norm_g, w_q, sub_keys, tm):
    t, d = x.shape
    nq = 2 * PEER_HEADS * PEER_HALF
    nk = PEER_HEADS * PEER_KEYS
    sk = sub_keys.reshape(2 * PEER_HEADS, PEER_KEYS, PEER_HALF).astype(BF16)
    col = lambda rows: pl.BlockSpec((rows, tm), lambda i: (0, i))
    return pl.pallas_call(
        _peer_route_kernel,
        grid=(t // tm,),
        in_specs=[pl.BlockSpec((tm, d), lambda i: (i, 0)),
                  pl.BlockSpec((1, d), lambda i: (0, 0)),
                  pl.BlockSpec((d, nq), lambda i: (0, 0)),
                  pl.BlockSpec((2 * PEER_HEADS, PEER_KEYS, PEER_HALF), lambda i: (0, 0, 0))],
        out_specs=[col(d), col(nk), col(nk), col(nk), col(nk)],
        out_shape=[jax.ShapeDtypeStruct((d, t), BF16)] + [jax.ShapeDtypeStruct((nk, t), F32)] * 4,
        scratch_shapes=[pltpu.VMEM((2 * nk, tm), F32)],
        compiler_params=_cparams(("parallel",), 56),
        name="peer_route",
    )(x, norm_g.reshape(1, d), w_q.astype(BF16), sk)


def _gelu(x):
    return 0.5 * x * (1.0 + lax.erf(x * (2.0 ** -0.5)))


def _peer_gated_act(blk, half, js, nr, xu_ref, w_ref, thr_ref, e0_ref, s1_ref, e1_ref):
    K = PEER_KEYS
    tmh = xu_ref.shape[1]
    for j in js:
        r = jnp.minimum(blk * nr + j, K - 1)
        thr_rows = [thr_ref[pl.ds(h * K + r, 1), :] for h in range(PEER_HEADS)]
        e0_rows = [e0_ref[pl.ds(h * K + r, 1), :] for h in range(PEER_HEADS)]
        for c in range(tmh // LANES):
            cl = slice(c * LANES, (c + 1) * LANES)
            cs = slice(half * tmh + c * LANES, half * tmh + (c + 1) * LANES)
            gate = None
            for h in range(PEER_HEADS):
                part = jnp.where(s1_ref[h * K:(h + 1) * K, cs] >= thr_rows[h][:, cs],
                                 e1_ref[h * K:(h + 1) * K, cs] * e0_rows[h][:, cs], 0.0)
                gate = part if gate is None else gate + part
            w_ref[j * K:(j + 1) * K, cl] = (_gelu(xu_ref[j * K:(j + 1) * K, cl]) * gate).astype(BF16)


def _peer_dense_kernel(xnt_ref, thr_ref, e0_ref, s1_ref, e1_ref, u_ref, vta_ref, vtb_ref, o_ref,
                       xu0_ref, xu1_ref, w0_ref, w1_ref, *, nr):
    g = pl.program_id(1)
    last = pl.num_programs(1) - 1
    eb = nr * PEER_KEYS
    tmh = xnt_ref.shape[1] // 2
    xu, w = (xu0_ref, xu1_ref), (w0_ref, w1_ref)

    def score(s):
        hd = s % 2
        xu[hd][...] = jnp.dot(u_ref[(s // 2) * eb:(s // 2 + 1) * eb, :], xnt_ref[:, hd * tmh:(hd + 1) * tmh],
                              preferred_element_type=F32)

    def value(s):
        hd = s % 2
        vt_ref = vtb_ref if s < 2 else vta_ref
        o_ref[:, hd * tmh:(hd + 1) * tmh] += jnp.dot(vt_ref[...], w[hd][...], preferred_element_type=F32)

    def act(s, js):
        hv = (s + 1) % 2
        _peer_gated_act(2 * g + (s - 1) // 2, hv, js, nr, xu[hv], w[hv], thr_ref, e0_ref, s1_ref, e1_ref)

    def slot(s):
        score(s)
        act(s, range(0, nr // 2))
        value(s)
        act(s, range(nr // 2, nr))

    @pl.when(g == 0)
    def _():
        o_ref[...] = jnp.zeros(o_ref.shape, F32)
        score(0)
        score(1)
        act(1, range(nr))

    @pl.when(g > 0)
    def _():
        slot(0)
        slot(1)

    @pl.when(g < last)
    def _():
        slot(2)
        slot(3)


def peer_dense(xnt, thr, e0, s1, e1, u_tab, v_tab, tm, eb):
    d, t = xnt.shape
    n_exp = u_tab.shape[0]
    nblk = n_exp // eb
    nk = thr.shape[0]
    u16 = cast_bf16(u_tab, 1024)
    vt16 = cast_transpose_bf16(v_tab, 512)
    col = lambda rows: pl.BlockSpec((rows, tm), lambda i, g: (0, i))
    return pl.pallas_call(
        functools.partial(_peer_dense_kernel, nr=eb // PEER_KEYS),
        grid=(t // tm, nblk // 2 + 1),
        in_specs=[col(d), col(nk), col(nk), col(nk), col(nk),
                  pl.BlockSpec((2 * eb, d), lambda i, g: (jnp.minimum(g, nblk // 2 - 1), 0)),
                  pl.BlockSpec((d, eb), lambda i, g: (0, jnp.minimum(2 * g, nblk - 1))),
                  pl.BlockSpec((d, eb), lambda i, g: (0, jnp.maximum(2 * g - 1, 0)))],
        out_specs=pl.BlockSpec((d, tm), lambda i, g: (0, i)),
        out_shape=jax.ShapeDtypeStruct((d, t), F32),
        scratch_shapes=[pltpu.VMEM((eb, tm // 2), F32),
                        pltpu.VMEM((eb, tm // 2), F32),
                        pltpu.VMEM((eb, tm // 2), BF16),
                        pltpu.VMEM((eb, tm // 2), BF16)],
        compiler_params=_cparams(("parallel", "arbitrary"), 56),
        name="peer_dense",
    )(xnt, thr, e0, s1, e1, u16, vt16, vt16)


def _transpose_add_kernel(x_ref, pt_ref, o_ref):
    o_ref[...] = x_ref[...] + pt_ref[...].T


def transpose_add(x, pt, tm):
    t, d = x.shape
    return pl.pallas_call(
        _transpose_add_kernel,
        grid=(t // tm,),
        in_specs=[pl.BlockSpec((tm, d), lambda i: (i, 0)), pl.BlockSpec((d, tm), lambda i: (0, i))],
        out_specs=pl.BlockSpec((tm, d), lambda i: (i, 0)),
        out_shape=jax.ShapeDtypeStruct((t, d), F32),
        compiler_params=_cparams(("parallel",), 48),
        name="transpose_add",
    )(x, pt)


def _even_w_in(w_in):
    o = np.cumsum([0, MLA_Q_RANK, MLA_KV_RANK, MLA_ROPE, SSD_INNER, SSD_CONV_DIM, SSD_HEADS])
    c_q, c_kv, k_r, z, xbc, dt = (w_in[:, o[i]:o[i + 1]] for i in range(6))
    dt = jnp.pad(dt, ((0, 0), (0, LANES - SSD_HEADS)))
    return jnp.concatenate([xbc, z, c_q, c_kv, _pad_rope_cols(k_r), dt], axis=1).astype(BF16)


def _tile(n, pref):
    return pref if n % pref == 0 else n


def kernel(x, mem, positions, norm_mix, norm_mem, norm_mem_kv, norm_ffn, mem_w_q, mem_w_kv, mem_q_gain, mem_k_gain, mem_w_o, peer_w_q, peer_sub_keys, peer_u, peer_v, ev_w_in, ev_q_norm, ev_kv_norm, ev_w_uq, ev_w_ukv, ev_q_gain, ev_k_gain, ev_conv_w, ev_conv_b, ev_dt_bias, ev_a_log, ev_d_skip, ev_ssd_norm, ev_w_out, od_w_qkv, od_q_gain, od_k_gain, od_lam_q1, od_lam_k1, od_lam_q2, od_lam_k2, od_subln, od_w_o):
    bsz, s, d = x.shape
    t = bsz * s
    depth = norm_mix.shape[0]
    tm = _tile(s, 512)
    tq = _tile(s, 512)
    tm2 = _tile(t, 2 * tm)
    xf = x.reshape(t, d)
    pos = positions.reshape(t)
    mem_f = mem.reshape(-1, d)
    cos_mla, sin_mla = rope_tables(pos, MLA_ROPE // 2, LANES // 2)
    cos_dif, sin_dif = rope_tables(pos, DIFF_HEAD_DIM // 2, LANES // 2)

    for layer in range(depth):
        i = layer // 2
        if layer % 2 == 0:
            proj = norm_matmul(xf, norm_mix[layer], _even_w_in(ev_w_in[i]), tm, IN_TOTAL // 2)
            q, k, v = mla_prep(proj, ev_q_norm[i], ev_kv_norm[i], ev_w_uq[i], ev_w_ukv[i],
                               ev_q_gain[i], ev_k_gain[i], cos_mla, sin_mla, tm)
            mla_out = mla_attention(q, k, v, bsz, s, tq)
            y = ssd_mixer(proj, ev_conv_w[i], ev_conv_b[i], ev_dt_bias[i], ev_a_log[i], ev_d_skip[i],
                          ev_ssd_norm[i], bsz, s)
            w_out = ev_w_out[i].astype(BF16)
            n_mla = MLA_HEADS * MLA_V
            xf = matmul_residual([mla_out, y], [w_out[:n_mla], w_out[n_mla:]], xf, tm2, 1024)
        else:
            q, k, v = diff_qkv(xf, norm_mix[layer], od_w_qkv[i].astype(BF16), od_q_gain[i], od_k_gain[i],
                               cos_dif, sin_dif, tm)
            lam_init = 0.8 - 0.6 * math.exp(-0.3 * layer)
            attn = diff_attention(q, k, v, od_lam_q1[i], od_lam_k1[i], od_lam_q2[i], od_lam_k2[i],
                                  od_subln[i], lam_init, bsz, s, tq)
            xf = matmul_residual([attn], [od_w_o[i].astype(BF16)], xf, tm2, 1024)
        mem_kv = norm_matmul(mem_f, norm_mem_kv[layer], mem_w_kv[layer].astype(BF16),
                             _tile(mem_f.shape[0], 512), 2 * MEM_WIDTH)
        xf = memory_attention(xf, mem_kv, norm_mem[layer], mem_w_q[layer], mem_q_gain[layer],
                              mem_k_gain[layer], mem_w_o[layer], bsz, s, tm)
        routing = peer_route(xf, norm_ffn[layer], peer_w_q[layer], peer_sub_keys[layer], tm)
        xf = transpose_add(xf, peer_dense(*routing, peer_u[layer], peer_v[layer], tm, 512), tm)
    return xf.reshape(bsz, s, d)
```

```python
import functools
import math

import jax
import jax.numpy as jnp
import numpy as np
from jax import lax
from jax.experimental import pallas as pl
from jax.experimental.pallas import tpu as pltpu

F32 = jnp.float32
BF16 = jnp.bfloat16

EPS = 1e-6
ROPE_THETA = 10000.0
LANES = 128

MLA_HEADS = 8
MLA_Q_RANK = 512
MLA_KV_RANK = 512
MLA_NOPE = 128
MLA_ROPE = 64
MLA_V = 128
MLA_QK = MLA_NOPE + MLA_ROPE
MLA_QK_PAD = 256

SSD_HEADS = 16
SSD_HEAD_DIM = 64
SSD_INNER = SSD_HEADS * SSD_HEAD_DIM
SSD_GROUPS = 4
SSD_STATE = 128
SSD_CONV = 4
SSD_CHUNK = 128
SSD_CONV_DIM = SSD_INNER + 2 * SSD_GROUPS * SSD_STATE
SSD_HPG = SSD_HEADS // SSD_GROUPS

DIFF_HEADS = 8
DIFF_HEAD_DIM = 128
DIFF_V = 256

MEM_HEADS = 4
MEM_HEAD_DIM = 128
MEM_WIDTH = MEM_HEADS * MEM_HEAD_DIM

PEER_HEADS = 8
PEER_KEYS = 128
PEER_TOPK = 16
PEER_HALF = 128

IN_XBC = 0
IN_Z = IN_XBC + SSD_CONV_DIM
IN_CQ = IN_Z + SSD_INNER
IN_CKV = IN_CQ + MLA_Q_RANK
IN_KR = IN_CKV + MLA_KV_RANK
IN_DT = IN_KR + LANES
IN_TOTAL = IN_DT + LANES

NEG_BIG = -1e30


def _cparams(sem, vmem_mb):
    return pltpu.CompilerParams(dimension_semantics=sem, vmem_limit_bytes=vmem_mb * 1024 * 1024)


def _rms(x, g, n=None):
    ss = jnp.sum(x * x, axis=-1, keepdims=True)
    n = x.shape[-1] if n is None else n
    return x * lax.rsqrt(ss * (1.0 / n) + EPS) * g


def _dot_nt(a, b):
    return lax.dot_general(a, b, (((1,), (1,)), ((), ())), preferred_element_type=F32)


def _norm_mm_kernel(x_ref, g_ref, w_ref, o_ref, xn_ref):
    @pl.when(pl.program_id(1) == 0)
    def _():
        xn_ref[...] = _rms(x_ref[...], g_ref[...]).astype(BF16)

    o_ref[...] = jnp.dot(xn_ref[...], w_ref[...], preferred_element_type=F32).astype(o_ref.dtype)


def norm_matmul(x, g, w, tm, tn, out_dtype=F32):
    t, d = x.shape
    n = w.shape[1]
    return pl.pallas_call(
        _norm_mm_kernel,
        grid=(t // tm, n // tn),
        in_specs=[pl.BlockSpec((tm, d), lambda i, j: (i, 0)),
                  pl.BlockSpec((1, d), lambda i, j: (0, 0)),
                  pl.BlockSpec((d, tn), lambda i, j: (0, j))],
        out_specs=pl.BlockSpec((tm, tn), lambda i, j: (i, j)),
        out_shape=jax.ShapeDtypeStruct((t, n), out_dtype),
        scratch_shapes=[pltpu.VMEM((tm, d), BF16)],
        compiler_params=_cparams(("parallel", "arbitrary"), 56),
        name="norm_matmul",
    )(x, g.reshape(1, d), w)


def _mm_res_kernel(*refs, n_a):
    res_ref, o_ref = refs[2 * n_a], refs[2 * n_a + 1]
    acc = res_ref[...]
    for a_ref, w_ref in zip(refs[:n_a], refs[n_a:2 * n_a]):
        acc = acc + jnp.dot(a_ref[...], w_ref[...], preferred_element_type=F32)
    o_ref[...] = acc


def matmul_residual(a_list, w_list, res, tm, tn):
    t, n = res.shape
    n_a = len(a_list)
    in_specs = ([pl.BlockSpec((tm, a.shape[1]), lambda i, j: (i, 0)) for a in a_list]
                + [pl.BlockSpec((w.shape[0], tn), lambda i, j: (0, j)) for w in w_list]
                + [pl.BlockSpec((tm, tn), lambda i, j: (i, j))])
    return pl.pallas_call(
        functools.partial(_mm_res_kernel, n_a=n_a),
        grid=(t // tm, n // tn),
        in_specs=in_specs,
        out_specs=pl.BlockSpec((tm, tn), lambda i, j: (i, j)),
        out_shape=jax.ShapeDtypeStruct((t, n), F32),
        compiler_params=_cparams(("parallel", "parallel"), 56),
        name="matmul_residual",
    )(*a_list, *w_list, res)


def _cast_kernel(x_ref, o_ref):
    o_ref[...] = x_ref[...].astype(o_ref.dtype)


def _cast_t_kernel(x_ref, o_ref):
    o_ref[...] = x_ref[...].T.astype(o_ref.dtype)


def cast_bf16(x, layer, tr):
    _, r, c = x.shape
    return pl.pallas_call(
        _cast_kernel,
        grid=(r // tr,),
        in_specs=[pl.BlockSpec((None, tr, c), lambda i: (layer, i, 0))],
        out_specs=pl.BlockSpec((tr, c), lambda i: (i, 0)),
        out_shape=jax.ShapeDtypeStruct((r, c), BF16),
        compiler_params=_cparams(("parallel",), 32),
        name="cast_bf16",
    )(x)


def cast_transpose_bf16(x, layer, tr):
    _, r, c = x.shape
    return pl.pallas_call(
        _cast_t_kernel,
        grid=(r // tr,),
        in_specs=[pl.BlockSpec((None, tr, c), lambda i: (layer, i, 0))],
        out_specs=pl.BlockSpec((c, tr), lambda i: (0, i)),
        out_shape=jax.ShapeDtypeStruct((c, r), BF16),
        compiler_params=_cparams(("parallel",), 32),
        name="cast_transpose_bf16",
    )(x)


def _rope_tab_kernel(pos_ref, inv_ref, sgn_ref, cos_ref, sin_ref):
    ang = pos_ref[...].astype(F32) * inv_ref[...]
    cos_ref[...] = jnp.cos(ang)
    sin_ref[...] = jnp.sin(ang) * sgn_ref[...]


def rope_tables(pos, half, stride):
    t = pos.shape[0]
    lane = np.arange(LANES)
    j = lane % stride
    inv = np.where(j < half, 1.0 / (ROPE_THETA ** (np.minimum(j, half - 1).astype(np.float32) / half)), 0.0)
    sgn = np.where(lane < stride, -1.0, 1.0)
    tm = min(t, 1024)
    return pl.pallas_call(
        _rope_tab_kernel,
        grid=(t // tm,),
        in_specs=[pl.BlockSpec((tm, 1), lambda i: (i, 0)),
                  pl.BlockSpec((1, LANES), lambda i: (0, 0)),
                  pl.BlockSpec((1, LANES), lambda i: (0, 0))],
        out_specs=[pl.BlockSpec((tm, LANES), lambda i: (i, 0))] * 2,
        out_shape=[jax.ShapeDtypeStruct((t, LANES), F32)] * 2,
        compiler_params=_cparams(("parallel",), 32),
        name="rope_tables",
    )(pos.reshape(t, 1), jnp.asarray(inv, F32).reshape(1, LANES), jnp.asarray(sgn, F32).reshape(1, LANES))


def _rope_seg(x, cos, sin):
    return x * cos + pltpu.roll(x, LANES // 2, 1) * sin


def _rep(x, n):
    return x if n == 1 else jnp.concatenate([x] * n, axis=1)


def _flash(chains, n_full, tq, tk):
    for (_, _, _, _, _, _, _, m_ref, l_ref, acc_ref) in chains:
        m_ref[...] = jnp.full(m_ref.shape, NEG_BIG, F32)
        l_ref[...] = jnp.zeros(l_ref.shape, F32)
        acc_ref[...] = jnp.zeros(acc_ref.shape, F32)

    def step(j, masked):
        start = pl.multiple_of(j * tk, tk)
        if masked:
            row = lax.broadcasted_iota(jnp.int32, (tq, tk), 0)
            col = lax.broadcasted_iota(jnp.int32, (tq, tk), 1)
            keep = col <= row
        for (q, k_ref, koff, dk, v_ref, voff, dv, m_ref, l_ref, acc_ref) in chains:
            s = _dot_nt(q, k_ref[pl.ds(start, tk), koff:koff + dk])
            if masked:
                s = jnp.where(keep, s, NEG_BIG)
            m_prev = m_ref[...]
            m_new = jnp.maximum(m_prev, jnp.max(s, axis=-1, keepdims=True))
            p = jnp.exp(s - _rep(m_new, tk // LANES))
            alpha = jnp.exp(m_prev - m_new)
            l_ref[...] = alpha * l_ref[...] + jnp.sum(p, axis=-1, keepdims=True)
            acc_ref[...] = (_rep(alpha, dv // LANES) * acc_ref[...]
                            + jnp.dot(p.astype(BF16), v_ref[pl.ds(start, tk), voff:voff + dv],
                                      preferred_element_type=F32))
            m_ref[...] = m_new

    def body(j, carry):
        step(j, False)
        return carry

    lax.fori_loop(0, n_full, body, 0)
    step(n_full, True)
    return [acc_ref[...] / _rep(l_ref[...], dv // LANES) for (_, _, _, _, _, _, dv, _, l_ref, acc_ref) in chains]


def _mla_prep_kernel(cq_ref, ckv_ref, kr_ref, qn_ref, kvn_ref, wuq_ref, wukv_ref, qg_ref, kg_ref,
                     cos_ref, sin_ref, q_out, k_out, v_out):
    cq = _rms(cq_ref[...], qn_ref[...]).astype(BF16)
    qf = jnp.dot(cq, wuq_ref[...], preferred_element_type=F32)
    ckv = _rms(ckv_ref[...], kvn_ref[...]).astype(BF16)
    kvf = jnp.dot(ckv, wukv_ref[...], preferred_element_type=F32)
    kr = kr_ref[...]
    kr_ss = jnp.sum(kr * kr, axis=-1, keepdims=True)
    cos, sin = cos_ref[...], sin_ref[...]
    qg, kg = qg_ref[...], kg_ref[...]
    scale = MLA_QK ** -0.5
    for h in range(MLA_HEADS):
        lo = h * MLA_QK_PAD
        qh = _rms(qf[:, lo:lo + MLA_QK_PAD], qg, MLA_QK) * scale
        q_out[:, lo:lo + LANES] = qh[:, :LANES].astype(BF16)
        q_out[:, lo + LANES:lo + 2 * LANES] = _rope_seg(qh[:, LANES:], cos, sin).astype(BF16)
        kn = kvf[:, lo:lo + MLA_NOPE]
        r = lax.rsqrt((jnp.sum(kn * kn, axis=-1, keepdims=True) + kr_ss) * (1.0 / MLA_QK) + EPS)
        k_out[:, lo:lo + LANES] = (kn * r * kg[:, :LANES]).astype(BF16)
        k_out[:, lo + LANES:lo + 2 * LANES] = _rope_seg(kr * r * kg[:, LANES:], cos, sin).astype(BF16)
        v_out[:, h * MLA_V:(h + 1) * MLA_V] = kvf[:, lo + MLA_NOPE:lo + MLA_QK_PAD].astype(BF16)


def _pad_rope_cols(w):
    half = MLA_ROPE // 2
    z = jnp.zeros(w.shape[:-1] + (half,), w.dtype)
    return jnp.concatenate([w[..., :half], z, w[..., half:], z], axis=-1)


def mla_prep(proj, q_norm, kv_norm, w_uq, w_ukv, q_gain, k_gain, cos, sin, tm):
    t = proj.shape[0]
    hq = MLA_HEADS * MLA_QK_PAD
    wq = w_uq.reshape(MLA_Q_RANK, MLA_HEADS, MLA_QK)
    wq = jnp.concatenate([wq[..., :MLA_NOPE], _pad_rope_cols(wq[..., MLA_NOPE:])], axis=-1)
    wq = wq.reshape(MLA_Q_RANK, hq).astype(BF16)
    qg = jnp.concatenate([q_gain[:MLA_NOPE], _pad_rope_cols(q_gain[MLA_NOPE:])]).reshape(1, MLA_QK_PAD)
    kg = jnp.concatenate([k_gain[:MLA_NOPE], _pad_rope_cols(k_gain[MLA_NOPE:])]).reshape(1, MLA_QK_PAD)
    const = lambda shape: pl.BlockSpec(shape, lambda i: (0, 0))
    return pl.pallas_call(
        _mla_prep_kernel,
        grid=(t // tm,),
        in_specs=[pl.BlockSpec((tm, MLA_Q_RANK), lambda i: (i, IN_CQ // MLA_Q_RANK)),
                  pl.BlockSpec((tm, MLA_KV_RANK), lambda i: (i, IN_CKV // MLA_KV_RANK)),
                  pl.BlockSpec((tm, LANES), lambda i: (i, IN_KR // LANES)),
                  const((1, MLA_Q_RANK)), const((1, MLA_KV_RANK)),
                  const((MLA_Q_RANK, hq)), const((MLA_KV_RANK, hq)),
                  const((1, MLA_QK_PAD)), const((1, MLA_QK_PAD)),
                  pl.BlockSpec((tm, LANES), lambda i: (i, 0)),
                  pl.BlockSpec((tm, LANES), lambda i: (i, 0))],
        out_specs=[pl.BlockSpec((tm, hq), lambda i: (i, 0)),
                   pl.BlockSpec((tm, hq), lambda i: (i, 0)),
                   pl.BlockSpec((tm, MLA_HEADS * MLA_V), lambda i: (i, 0))],
        out_shape=[jax.ShapeDtypeStruct((t, hq), BF16), jax.ShapeDtypeStruct((t, hq), BF16),
                   jax.ShapeDtypeStruct((t, MLA_HEADS * MLA_V), BF16)],
        compiler_params=_cparams(("parallel",), 56),
        name="mla_prep",
    )(proj, proj, proj, q_norm.reshape(1, -1), kv_norm.reshape(1, -1), wq, w_ukv.astype(BF16), qg, kg, cos, sin)


MLA_HPS = 4
DIFF_HPS = 4


def _mla_attn_kernel(q_ref, k_ref, v_ref, o_ref, m_ref, l_ref, acc_ref, *, tq):
    chains = [(q_ref[:, h * MLA_QK_PAD:(h + 1) * MLA_QK_PAD], k_ref, h * MLA_QK_PAD, MLA_QK_PAD,
               v_ref, h * MLA_V, MLA_V, m_ref.at[h], l_ref.at[h], acc_ref.at[h]) for h in range(MLA_HPS)]
    outs = _flash(chains, pl.program_id(2), tq, tq)
    for h in range(MLA_HPS):
        o_ref[:, h * MLA_V:(h + 1) * MLA_V] = outs[h].astype(o_ref.dtype)


def mla_attention(q, k, v, bsz, s, tq):
    hq = MLA_HEADS * MLA_QK_PAD
    q, k = q.reshape(bsz, s, hq), k.reshape(bsz, s, hq)
    v = v.reshape(bsz, s, MLA_HEADS * MLA_V)
    hps = MLA_HPS
    out = pl.pallas_call(
        functools.partial(_mla_attn_kernel, tq=tq),
        grid=(bsz, MLA_HEADS // hps, s // tq),
        in_specs=[pl.BlockSpec((None, tq, hps * MLA_QK_PAD), lambda b, h, i: (b, i, h)),
                  pl.BlockSpec((None, s, hps * MLA_QK_PAD), lambda b, h, i: (b, 0, h)),
                  pl.BlockSpec((None, s, hps * MLA_V), lambda b, h, i: (b, 0, h))],
        out_specs=pl.BlockSpec((None, tq, hps * MLA_V), lambda b, h, i: (b, i, h)),
        out_shape=jax.ShapeDtypeStruct((bsz, s, MLA_HEADS * MLA_V), BF16),
        scratch_shapes=[pltpu.VMEM((hps, tq, LANES), F32), pltpu.VMEM((hps, tq, LANES), F32),
                        pltpu.VMEM((hps, tq, MLA_V), F32)],
        compiler_params=_cparams(("parallel", "parallel", "arbitrary"), 48),
        name="mla_attention",
    )(q, k, v)
    return out.reshape(bsz * s, MLA_HEADS * MLA_V)


def _silu(x):
    return x * (1.0 / (1.0 + jnp.exp(-x)))


def _softplus(x):
    return jnp.maximum(x, 0.0) + jnp.log(1.0 + jnp.exp(-jnp.abs(x)))


def _ssd_kernel(xbc_ref, z_ref, dt_ref, cw_ref, cb_ref, dtb_ref, alog_ref, dsk_ref, nrm_ref, y_ref,
                xc_ref, st_ref):
    c = pl.program_id(1)
    L, P, N, G, HPG = SSD_CHUNK, SSD_HEAD_DIM, SSD_STATE, SSD_GROUPS, SSD_HPG
    TAIL = 8

    @pl.when(c == 0)
    def _():
        xc_ref[0:TAIL, :] = jnp.zeros((TAIL, SSD_CONV_DIM), F32)
        st_ref[...] = jnp.zeros(st_ref.shape, F32)

    xc_ref[TAIL:TAIL + L, :] = xbc_ref[...]
    conv = cb_ref[...] + jnp.zeros((L, SSD_CONV_DIM), F32)
    for k in range(SSD_CONV):
        off = TAIL - (SSD_CONV - 1) + k
        conv = conv + xc_ref[off:off + L, :] * cw_ref[k:k + 1, :]
    xc_ref[0:TAIL, :] = xc_ref[L:L + TAIL, :]
    xbc = _silu(conv)

    dt = _softplus(dt_ref[...] + dtb_ref[...])
    adt = dt * (-jnp.exp(alog_ref[...]))
    li = lax.broadcasted_iota(jnp.int32, (L, L), 0)
    si = lax.broadcasted_iota(jnp.int32, (L, L), 1)
    tri = (si <= li).astype(F32)
    a_cs = jnp.dot(tri, adt, preferred_element_type=F32, precision=lax.Precision.HIGHEST)
    a_cs_t = a_cs.T
    a_last = a_cs[L - 1:L, :]
    dec_state = jnp.exp(a_last - a_cs)
    dec_out = jnp.exp(a_cs)
    dec_chunk = jnp.exp(a_last)
    causal = si <= li

    zs = _silu(z_ref[...])
    for g in range(G):
        bg = xbc[:, SSD_INNER + g * N:SSD_INNER + (g + 1) * N]
        cg = xbc[:, SSD_INNER + G * N + g * N:SSD_INNER + G * N + (g + 1) * N]
        bg16, cg16 = bg.astype(BF16), cg.astype(BF16)
        cb = _dot_nt(cg16, bg16)
        bgt16 = bg.T.astype(BF16)
        xs_g = xbc[:, g * HPG * P:(g + 1) * HPG * P]
        xdd, ydiag = [], []
        for hh in range(HPG):
            h = g * HPG + hh
            xs = xs_g[:, hh * P:(hh + 1) * P]
            xd = xs * dt[:, h:h + 1]
            seg = a_cs[:, h:h + 1] - a_cs_t[h:h + 1, :]
            lm = cb * jnp.exp(jnp.where(causal, seg, NEG_BIG))
            ydiag.append(jnp.dot(lm.astype(BF16), xd.astype(BF16), preferred_element_type=F32))
            xdd.append((xd * dec_state[:, h:h + 1]).astype(BF16))
        prev = st_ref[g]
        yoff = jnp.dot(cg16, prev.astype(BF16), preferred_element_type=F32)
        new = jnp.dot(bgt16, jnp.concatenate(xdd, axis=1), preferred_element_type=F32)
        ys, decs = [], []
        for hh in range(HPG):
            h = g * HPG + hh
            xs = xs_g[:, hh * P:(hh + 1) * P]
            y = ydiag[hh] + yoff[:, hh * P:(hh + 1) * P] * dec_out[:, h:h + 1] + dsk_ref[:, h:h + 1] * xs
            ys.append(y)
            decs.append(jnp.broadcast_to(dec_chunk[:, h:h + 1], (N, P)))
        st_ref[g] = prev * jnp.concatenate(decs, axis=1) + new
        yg = jnp.concatenate(ys, axis=1) * zs[:, g * HPG * P:(g + 1) * HPG * P]
        y_ref[:, g * HPG * P:(g + 1) * HPG * P] = _rms(yg, nrm_ref[:, g * HPG * P:(g + 1) * HPG * P]).astype(y_ref.dtype)


def _pad_lanes(v):
    return jnp.pad(v.astype(F32), (0, LANES - v.shape[0])).reshape(1, LANES)


def ssd_mixer(proj, conv_w, conv_b, dt_bias, a_log, d_skip, ssd_norm, bsz, s):
    nc = s // SSD_CHUNK
    const = lambda shape: pl.BlockSpec(shape, lambda b, c: (0, 0))
    return pl.pallas_call(
        _ssd_kernel,
        grid=(bsz, nc),
        in_specs=[pl.BlockSpec((SSD_CHUNK, SSD_CONV_DIM), lambda b, c: (b * nc + c, IN_XBC // SSD_CONV_DIM)),
                  pl.BlockSpec((SSD_CHUNK, SSD_INNER), lambda b, c: (b * nc + c, IN_Z // SSD_INNER)),
                  pl.BlockSpec((SSD_CHUNK, LANES), lambda b, c: (b * nc + c, IN_DT // LANES)),
                  const((SSD_CONV, SSD_CONV_DIM)), const((1, SSD_CONV_DIM)),
                  const((1, LANES)), const((1, LANES)), const((1, LANES)), const((1, SSD_INNER))],
        out_specs=pl.BlockSpec((SSD_CHUNK, SSD_INNER), lambda b, c: (b * nc + c, 0)),
        out_shape=jax.ShapeDtypeStruct((bsz * s, SSD_INNER), BF16),
        scratch_shapes=[pltpu.VMEM((SSD_CHUNK + 8, SSD_CONV_DIM), F32),
                        pltpu.VMEM((SSD_GROUPS, SSD_STATE, SSD_HPG * SSD_HEAD_DIM), F32)],
        compiler_params=_cparams(("parallel", "arbitrary"), 48),
        name="ssd_mixer",
    )(proj, proj, proj, conv_w.astype(F32), conv_b.reshape(1, -1).astype(F32),
      _pad_lanes(dt_bias), _pad_lanes(a_log), _pad_lanes(d_skip), ssd_norm.reshape(1, -1).astype(F32))


def _diff_qkv_kernel(x_ref, g_ref, w_ref, qg_ref, kg_ref, cos_ref, sin_ref, q_out, k_out, v_out, xn_ref):
    j = pl.program_id(1)

    @pl.when(j == 0)
    def _():
        xn_ref[...] = _rms(x_ref[...], g_ref[...]).astype(BF16)

    acc = jnp.dot(xn_ref[...], w_ref[...], preferred_element_type=F32)

    def normed(gain_ref, scale, out):
        cos, sin = cos_ref[...], sin_ref[...]
        for m in range(2 * DIFF_HEADS):
            lo = m * DIFF_HEAD_DIM
            y = _rms(acc[:, lo:lo + DIFF_HEAD_DIM], gain_ref[...]) * scale
            out[:, lo:lo + DIFF_HEAD_DIM] = _rope_seg(y, cos, sin).astype(BF16)

    @pl.when(j == 0)
    def _():
        normed(qg_ref, DIFF_HEAD_DIM ** -0.5, q_out)

    @pl.when(j == 1)
    def _():
        normed(kg_ref, 1.0, k_out)

    @pl.when(j == 2)
    def _():
        v_out[...] = acc.astype(BF16)


def diff_qkv(x, norm_g, w_qkv, q_gain, k_gain, cos, sin, tm):
    t, dm = x.shape
    d = 2 * DIFF_HEADS * DIFF_HEAD_DIM
    const = lambda shape: pl.BlockSpec(shape, lambda i, j: (0, 0))
    return pl.pallas_call(
        _diff_qkv_kernel,
        grid=(t // tm, 3),
        in_specs=[pl.BlockSpec((tm, dm), lambda i, j: (i, 0)),
                  const((1, dm)),
                  pl.BlockSpec((dm, d), lambda i, j: (0, j)),
                  const((1, DIFF_HEAD_DIM)), const((1, DIFF_HEAD_DIM)),
                  pl.BlockSpec((tm, LANES), lambda i, j: (i, 0)),
                  pl.BlockSpec((tm, LANES), lambda i, j: (i, 0))],
        out_specs=[pl.BlockSpec((tm, d), lambda i, j: (i, 0))] * 3,
        out_shape=[jax.ShapeDtypeStruct((t, d), BF16)] * 3,
        scratch_shapes=[pltpu.VMEM((tm, dm), BF16)],
        compiler_params=_cparams(("parallel", "arbitrary"), 56),
        name="diff_qkv",
    )(x, norm_g.reshape(1, dm), w_qkv, q_gain.reshape(1, -1), k_gain.reshape(1, -1), cos, sin)


def _diff_attn_kernel(q_ref, k_ref, v_ref, lq1_ref, lk1_ref, lq2_ref, lk2_ref, sub_ref, o_ref,
                      m_ref, l_ref, acc_ref, *, tq, lam_init):
    dh = DIFF_HEAD_DIM
    chains = [(q_ref[:, c * dh:(c + 1) * dh], k_ref, c * dh, dh, v_ref, (c // 2) * DIFF_V, DIFF_V,
               m_ref.at[c], l_ref.at[c], acc_ref.at[c]) for c in range(2 * DIFF_HPS)]
    outs = _flash(chains, pl.program_id(2), tq, tq)
    lam = (jnp.exp(jnp.sum(lq1_ref[...] * lk1_ref[...], axis=-1, keepdims=True))
           - jnp.exp(jnp.sum(lq2_ref[...] * lk2_ref[...], axis=-1, keepdims=True)) + lam_init)
    for h in range(DIFF_HPS):
        attn = outs[2 * h] - lam * outs[2 * h + 1]
        o_ref[:, h * DIFF_V:(h + 1) * DIFF_V] = (_rms(attn, sub_ref[...]) * (1.0 - lam_init)).astype(o_ref.dtype)


def diff_attention(q, k, v, lam_q1, lam_k1, lam_q2, lam_k2, subln, lam_init, bsz, s, tq):
    d = 2 * DIFF_HEADS * DIFF_HEAD_DIM
    q, k, v = q.reshape(bsz, s, d), k.reshape(bsz, s, d), v.reshape(bsz, s, d)
    hps = DIFF_HPS
    const = lambda shape: pl.BlockSpec(shape, lambda b, h, i: (0, 0))
    out = pl.pallas_call(
        functools.partial(_diff_attn_kernel, tq=tq, lam_init=lam_init),
        grid=(bsz, DIFF_HEADS // hps, s // tq),
        in_specs=[pl.BlockSpec((None, tq, hps * DIFF_V), lambda b, h, i: (b, i, h)),
                  pl.BlockSpec((None, s, hps * DIFF_V), lambda b, h, i: (b, 0, h)),
                  pl.BlockSpec((None, s, hps * DIFF_V), lambda b, h, i: (b, 0, h)),
                  const((1, DIFF_HEAD_DIM)), const((1, DIFF_HEAD_DIM)),
                  const((1, DIFF_HEAD_DIM)), const((1, DIFF_HEAD_DIM)), const((1, DIFF_V))],
        out_specs=pl.BlockSpec((None, tq, hps * DIFF_V), lambda b, h, i: (b, i, h)),
        out_shape=jax.ShapeDtypeStruct((bsz, s, d), BF16),
        scratch_shapes=[pltpu.VMEM((2 * hps, tq, LANES), F32), pltpu.VMEM((2 * hps, tq, LANES), F32),
                        pltpu.VMEM((2 * hps, tq, DIFF_V), F32)],
        compiler_params=_cparams(("parallel", "parallel", "arbitrary"), 56),
        name="diff_attention",
    )(q, k, v, lam_q1.reshape(1, -1), lam_k1.reshape(1, -1), lam_q2.reshape(1, -1), lam_k2.reshape(1, -1),
      subln.reshape(1, -1))
    return out.reshape(bsz * s, d)


def _mem_attn_kernel(x_ref, g_ref, wq_ref, kv_ref, qg_ref, kg_ref, wo_ref, o_ref):
    x = x_ref[...]
    xn = _rms(x, g_ref[...]).astype(BF16)
    q = jnp.dot(xn, wq_ref[...], preferred_element_type=F32)
    kv = kv_ref[...]
    outs = []
    for h in range(MEM_HEADS):
        lo = h * MEM_HEAD_DIM
        qh = (_rms(q[:, lo:lo + MEM_HEAD_DIM], qg_ref[...]) * (MEM_HEAD_DIM ** -0.5)).astype(BF16)
        kh = _rms(kv[:, lo:lo + MEM_HEAD_DIM], kg_ref[...]).astype(BF16)
        vh = kv[:, MEM_WIDTH + lo:MEM_WIDTH + lo + MEM_HEAD_DIM].astype(BF16)
        s = _dot_nt(qh, kh)
        p = jnp.exp(s - jnp.max(s, axis=-1, keepdims=True))
        o = jnp.dot(p.astype(BF16), vh, preferred_element_type=F32)
        outs.append((o / jnp.sum(p, axis=-1, keepdims=True)).astype(BF16))
    o_ref[...] = x + jnp.dot(jnp.concatenate(outs, axis=1), wo_ref[...], preferred_element_type=F32)


def memory_attention(x, mem_kv, norm_g, w_q, q_gain, k_gain, w_o, bsz, s, tm):
    t, d = x.shape
    m = mem_kv.shape[0] // bsz
    nb = s // tm
    const = lambda shape: pl.BlockSpec(shape, lambda i: (0, 0))
    return pl.pallas_call(
        _mem_attn_kernel,
        grid=(t // tm,),
        in_specs=[pl.BlockSpec((tm, d), lambda i: (i, 0)),
                  const((1, d)), const((d, MEM_WIDTH)),
                  pl.BlockSpec((m, 2 * MEM_WIDTH), lambda i: (i // nb, 0)),
                  const((1, MEM_HEAD_DIM)), const((1, MEM_HEAD_DIM)), const((MEM_WIDTH, d))],
        out_specs=pl.BlockSpec((tm, d), lambda i: (i, 0)),
        out_shape=jax.ShapeDtypeStruct((t, d), F32),
        compiler_params=_cparams(("parallel",), 56),
        name="memory_attention",
    )(x, norm_g.reshape(1, d), w_q.astype(BF16), mem_kv, q_gain.reshape(1, -1), k_gain.reshape(1, -1),
      w_o.astype(BF16))


def _compare_exchange(lst, i, j):
    a, b = lst[i], lst[j]
    if b is None:
        return
    if a is None:
        lst[i], lst[j] = b, None
        return
    lst[i], lst[j] = jnp.maximum(a, b), jnp.minimum(a, b)


def _bitonic_sort(lst):
    n = len(lst)
    k = 2
    while k <= n:
        j = k // 2
        while j >= 1:
            for i in range(n):
                l = i ^ j
                if l > i:
                    if i & k == 0:
                        _compare_exchange(lst, i, l)
                    else:
                        _compare_exchange(lst, l, i)
            j //= 2
        k *= 2


def _bitonic_merge(lst):
    j = len(lst) // 2
    while j >= 1:
        for i in range(len(lst)):
            if i ^ j > i:
                _compare_exchange(lst, i, i ^ j)
        j //= 2


def _top_values(x, k):
    sub = 8
    nb = x.shape[0] // sub
    size = 1
    while size < nb:
        size *= 2
    lst = [x[i * sub:(i + 1) * sub, :] for i in range(nb)] + [None] * (size - nb)
    _bitonic_sort(lst)
    lst = (lst + [None] * k)[:k]
    shift = sub // 2
    while shift >= 1:
        other = [None if v is None else pltpu.roll(v, shift, 0) for v in lst]
        merged = []
        for i in range(k):
            a, b = lst[i], other[k - 1 - i]
            merged.append(b if a is None else a if b is None else jnp.maximum(a, b))
        _bitonic_merge(merged)
        lst = merged
        shift //= 2
    return [v[0:1, :] for v in lst]


_PAIRS = [(i, j) for i in range(PEER_TOPK) for j in range(PEER_TOPK) if (i + 1) * (j + 1) <= PEER_TOPK]
_PAIR_ROWS = -(-len(_PAIRS) // 8) * 8


def _partner_threshold(s0, b, tau):
    hit = lambda v: s0 + v >= tau
    m8 = hit(b[7])
    thr = jnp.where(m8, b[7], jnp.inf)
    b4 = jnp.where(m8, b[11], b[3])
    m4 = hit(b4)
    thr = jnp.where(m4, b4, thr)
    b2 = jnp.where(m8, jnp.where(m4, b[13], b[9]), jnp.where(m4, b[5], b[1]))
    m2 = hit(b2)
    thr = jnp.where(m2, b2, thr)
    b1 = jnp.where(m8, jnp.where(m4, jnp.where(m2, b[14], b[12]), jnp.where(m2, b[10], b[8])),
                   jnp.where(m4, jnp.where(m2, b[6], b[4]), jnp.where(m2, b[2], b[0])))
    thr = jnp.where(hit(b1), b1, thr)
    return jnp.where(hit(b[15]), b[15], thr)


def _peer_route_kernel(x_ref, g_ref, wq_ref, sk_ref, xn_out, thr_out, e0_out, s1_out, e1_out, sc_ref):
    tm = x_ref.shape[0]
    K = PEER_KEYS
    RB = 64
    xn32 = _rms(x_ref[...], g_ref[...])
    xn_out[...] = xn32.T.astype(BF16)
    q = jnp.dot(xn32.astype(BF16), wq_ref[...], preferred_element_type=F32).astype(BF16)
    for g in range(2 * PEER_HEADS):
        sc_ref[g * K:(g + 1) * K, :] = _dot_nt(sk_ref[g], q[:, g * PEER_HALF:(g + 1) * PEER_HALF])
    for h in range(PEER_HEADS):
        lo0, lo1 = 2 * h * K, (2 * h + 1) * K
        a = _top_values(sc_ref[lo0:lo0 + K, :], PEER_TOPK)
        b = _top_values(sc_ref[lo1:lo1 + K, :], PEER_TOPK)
        cands = [a[i] + b[j] for i, j in _PAIRS]
        cands += [jnp.full((1, tm), -jnp.inf, F32)] * (_PAIR_ROWS - len(_PAIRS))
        cand = jnp.concatenate(cands, axis=0)
        tau = _top_values(cand, PEER_TOPK)[-1]
        z = jnp.sum(jnp.where(cand >= tau, jnp.exp(cand - (a[0] + b[0])), 0.0), axis=0, keepdims=True)
        for c in range(tm // LANES):
            cs = slice(c * LANES, (c + 1) * LANES)
            for rb in range(K // RB):
                s0 = sc_ref[lo0 + rb * RB:lo0 + (rb + 1) * RB, cs]
                thr_out[h * K + rb * RB:h * K + (rb + 1) * RB, cs] = _partner_threshold(
                    s0, [v[:, cs] for v in b], tau[:, cs])
        e0_out[h * K:(h + 1) * K, :] = jnp.exp(sc_ref[lo0:lo0 + K, :] - a[0])
        s1_out[h * K:(h + 1) * K, :] = sc_ref[lo1:lo1 + K, :]
        e1_out[h * K:(h + 1) * K, :] = jnp.exp(sc_ref[lo1:lo1 + K, :] - b[0]) * (1.0 / z)


def peer_route(x, norm_g, w_q, sub_keys, tm):
    t, d = x.shape
    nq = 2 * PEER_HEADS * PEER_HALF
    nk = PEER_HEADS * PEER_KEYS
    sk = sub_keys.reshape(2 * PEER_HEADS, PEER_KEYS, PEER_HALF).astype(BF16)
    col = lambda rows: pl.BlockSpec((rows, tm), lambda i: (0, i))
    return pl.pallas_call(
        _peer_route_kernel,
        grid=(t // tm,),
        in_specs=[pl.BlockSpec((tm, d), lambda i: (i, 0)),
                  pl.BlockSpec((1, d), lambda i: (0, 0)),
                  pl.BlockSpec((d, nq), lambda i: (0, 0)),
                  pl.BlockSpec((2 * PEER_HEADS, PEER_KEYS, PEER_HALF), lambda i: (0, 0, 0))],
        out_specs=[col(d), col(nk), col(nk), col(nk), col(nk)],
        out_shape=[jax.ShapeDtypeStruct((d, t), BF16)] + [jax.ShapeDtypeStruct((nk, t), F32)] * 4,
        scratch_shapes=[pltpu.VMEM((2 * nk, tm), F32)],
        compiler_params=_cparams(("parallel",), 56),
        name="peer_route",
    )(x, norm_g.reshape(1, d), w_q.astype(BF16), sk)


def _gelu(x):
    return 0.5 * x * (1.0 + lax.erf(x * (2.0 ** -0.5)))


def _peer_gated_act(blk, half, js, nr, xu_ref, w_ref, thr_ref, e0_ref, s1_ref, e1_ref):
    K = PEER_KEYS
    tmh = xu_ref.shape[1]
    for j in js:
        r = jnp.minimum(blk * nr + j, K - 1)
        thr_rows = [thr_ref[pl.ds(h * K + r, 1), :] for h in range(PEER_HEADS)]
        e0_rows = [e0_ref[pl.ds(h * K + r, 1), :] for h in range(PEER_HEADS)]
        for c in range(tmh // LANES):
            cl = slice(c * LANES, (c + 1) * LANES)
            cs = slice(half * tmh + c * LANES, half * tmh + (c + 1) * LANES)
            gate = None
            for h in range(PEER_HEADS):
                part = jnp.where(s1_ref[h * K:(h + 1) * K, cs] >= thr_rows[h][:, cs],
                                 e1_ref[h * K:(h + 1) * K, cs] * e0_rows[h][:, cs], 0.0)
                gate = part if gate is None else gate + part
            w_ref[j * K:(j + 1) * K, cl] = (_gelu(xu_ref[j * K:(j + 1) * K, cl]) * gate).astype(BF16)


def _peer_dense_kernel(xnt_ref, thr_ref, e0_ref, s1_ref, e1_ref, u_ref, vta_ref, vtb_ref, o_ref,
                       xu0_ref, xu1_ref, w0_ref, w1_ref, *, nr):
    g = pl.program_id(1)
    last = pl.num_programs(1) - 1
    eb = nr * PEER_KEYS
    tmh = xnt_ref.shape[1] // 2
    xu, w = (xu0_ref, xu1_ref), (w0_ref, w1_ref)

    def score(s):
        hd = s % 2
        xu[hd][...] = jnp.dot(u_ref[(s // 2) * eb:(s // 2 + 1) * eb, :], xnt_ref[:, hd * tmh:(hd + 1) * tmh],
                              preferred_element_type=F32)

    def value(s):
        hd = s % 2
        vt_ref = vtb_ref if s < 2 else vta_ref
        o_ref[:, hd * tmh:(hd + 1) * tmh] += jnp.dot(vt_ref[...], w[hd][...], preferred_element_type=F32)

    def act(s, js):
        hv = (s + 1) % 2
        _peer_gated_act(2 * g + (s - 1) // 2, hv, js, nr, xu[hv], w[hv], thr_ref, e0_ref, s1_ref, e1_ref)

    def slot(s):
        score(s)
        act(s, range(0, nr // 2))
        value(s)
        act(s, range(nr // 2, nr))

    @pl.when(g == 0)
    def _():
        o_ref[...] = jnp.zeros(o_ref.shape, F32)
        score(0)
        score(1)
        act(1, range(nr))

    @pl.when(g > 0)
    def _():
        slot(0)
        slot(1)

    @pl.when(g < last)
    def _():
        slot(2)
        slot(3)


def peer_dense(xnt, thr, e0, s1, e1, u_tab, v_tab, layer, tm, eb):
    d, t = xnt.shape
    n_exp = u_tab.shape[1]
    nblk = n_exp // eb
    nk = thr.shape[0]
    u16 = cast_bf16(u_tab, layer, 1024)
    vt16 = cast_transpose_bf16(v_tab, layer, 512)
    col = lambda rows: pl.BlockSpec((rows, tm), lambda i, g: (0, i))
    return pl.pallas_call(
        functools.partial(_peer_dense_kernel, nr=eb // PEER_KEYS),
        grid=(t // tm, nblk // 2 + 1),
        in_specs=[col(d), col(nk), col(nk), col(nk), col(nk),
                  pl.BlockSpec((2 * eb, d), lambda i, g: (jnp.minimum(g, nblk // 2 - 1), 0)),
                  pl.BlockSpec((d, eb), lambda i, g: (0, jnp.minimum(2 * g, nblk - 1))),
                  pl.BlockSpec((d, eb), lambda i, g: (0, jnp.maximum(2 * g - 1, 0)))],
        out_specs=pl.BlockSpec((d, tm), lambda i, g: (0, i)),
        out_shape=jax.ShapeDtypeStruct((d, t), F32),
        scratch_shapes=[pltpu.VMEM((eb, tm // 2), F32),
                        pltpu.VMEM((eb, tm // 2), F32),
                        pltpu.VMEM((eb, tm // 2), BF16),
                        pltpu.VMEM((eb, tm // 2), BF16)],
        compiler_params=_cparams(("parallel", "arbitrary"), 56),
        name="peer_dense",
    )(xnt, thr, e0, s1, e1, u16, vt16, vt16)


def _transpose_add_kernel(x_ref, pt_ref, o_ref):
    o_ref[...] = x_ref[...] + pt_ref[...].T


def transpose_add(x, pt, tm):
    t, d = x.shape
    return pl.pallas_call(
        _transpose_add_kernel,
        grid=(t // tm,),
        in_specs=[pl.BlockSpec((tm, d), lambda i: (i, 0)), pl.BlockSpec((d, tm), lambda i: (0, i))],
        out_specs=pl.BlockSpec((tm, d), lambda i: (i, 0)),
        out_shape=jax.ShapeDtypeStruct((t, d), F32),
        compiler_params=_cparams(("parallel",), 48),
        name="transpose_add",
    )(x, pt)


def _even_w_in(w_in):
    o = np.cumsum([0, MLA_Q_RANK, MLA_KV_RANK, MLA_ROPE, SSD_INNER, SSD_CONV_DIM, SSD_HEADS])
    c_q, c_kv, k_r, z, xbc, dt = (w_in[:, o[i]:o[i + 1]] for i in range(6))
    dt = jnp.pad(dt, ((0, 0), (0, LANES - SSD_HEADS)))
    return jnp.concatenate([xbc, z, c_q, c_kv, _pad_rope_cols(k_r), dt], axis=1).astype(BF16)


def _tile(n, pref):
    return pref if n % pref == 0 else n


def kernel(x, mem, positions, norm_mix, norm_mem, norm_mem_kv, norm_ffn, mem_w_q, mem_w_kv, mem_q_gain, mem_k_gain, mem_w_o, peer_w_q, peer_sub_keys, peer_u, peer_v, ev_w_in, ev_q_norm, ev_kv_norm, ev_w_uq, ev_w_ukv, ev_q_gain, ev_k_gain, ev_conv_w, ev_conv_b, ev_dt_bias, ev_a_log, ev_d_skip, ev_ssd_norm, ev_w_out, od_w_qkv, od_q_gain, od_k_gain, od_lam_q1, od_lam_k1, od_lam_q2, od_lam_k2, od_subln, od_w_o):
    bsz, s, d = x.shape
    t = bsz * s
    depth = norm_mix.shape[0]
    tm = _tile(s, 512)
    tq = _tile(s, 512)
    tm2 = _tile(t, 2 * tm)
    xf = x.reshape(t, d)
    pos = positions.reshape(t)
    mem_f = mem.reshape(-1, d)
    cos_mla, sin_mla = rope_tables(pos, MLA_ROPE // 2, LANES // 2)
    cos_dif, sin_dif = rope_tables(pos, DIFF_HEAD_DIM // 2, LANES // 2)

    for layer in range(depth):
        i = layer // 2
        if layer % 2 == 0:
            proj = norm_matmul(xf, norm_mix[layer], _even_w_in(ev_w_in[i]), tm, IN_TOTAL // 2)
            q, k, v = mla_prep(proj, ev_q_norm[i], ev_kv_norm[i], ev_w_uq[i], ev_w_ukv[i],
                               ev_q_gain[i], ev_k_gain[i], cos_mla, sin_mla, tm)
            mla_out = mla_attention(q, k, v, bsz, s, tq)
            y = ssd_mixer(proj, ev_conv_w[i], ev_conv_b[i], ev_dt_bias[i], ev_a_log[i], ev_d_skip[i],
                          ev_ssd_norm[i], bsz, s)
            w_out = ev_w_out[i].astype(BF16)
            n_mla = MLA_HEADS * MLA_V
            xf = matmul_residual([mla_out, y], [w_out[:n_mla], w_out[n_mla:]], xf, tm2, 1024)
        else:
            q, k, v = diff_qkv(xf, norm_mix[layer], od_w_qkv[i].astype(BF16), od_q_gain[i], od_k_gain[i],
                               cos_dif, sin_dif, tm)
            lam_init = 0.8 - 0.6 * math.exp(-0.3 * layer)
            attn = diff_attention(q, k, v, od_lam_q1[i], od_lam_k1[i], od_lam_q2[i], od_lam_k2[i],
                                  od_subln[i], lam_init, bsz, s, tq)
            xf = matmul_residual([attn], [od_w_o[i].astype(BF16)], xf, tm2, 1024)
        mem_kv = norm_matmul(mem_f, norm_mem_kv[layer], mem_w_kv[layer].astype(BF16),
                             _tile(mem_f.shape[0], 512), 2 * MEM_WIDTH)
        xf = memory_attention(xf, mem_kv, norm_mem[layer], mem_w_q[layer], mem_q_gain[layer],
                              mem_k_gain[layer], mem_w_o[layer], bsz, s, tm)
        routing = peer_route(xf, norm_ffn[layer], peer_w_q[layer], peer_sub_keys[layer], tm)
        xf = transpose_add(xf, peer_dense(*routing, peer_u, peer_v, layer, tm, 512), tm)
    return xf.reshape(bsz, s, d)
```

```python
import functools
import math

import jax
import jax.numpy as jnp
import numpy as np
from jax import lax
from jax.experimental import pallas as pl
from jax.experimental.pallas import tpu as pltpu

F32 = jnp.float32
BF16 = jnp.bfloat16

EPS = 1e-6
ROPE_THETA = 10000.0
LANES = 128

MLA_HEADS = 8
MLA_Q_RANK = 512
MLA_KV_RANK = 512
MLA_NOPE = 128
MLA_ROPE = 64
MLA_V = 128
MLA_QK = MLA_NOPE + MLA_ROPE
MLA_QK_PAD = 256

SSD_HEADS = 16
SSD_HEAD_DIM = 64
SSD_INNER = SSD_HEADS * SSD_HEAD_DIM
SSD_GROUPS = 4
SSD_STATE = 128
SSD_CONV = 4
SSD_CHUNK = 128
SSD_CONV_DIM = SSD_INNER + 2 * SSD_GROUPS * SSD_STATE
SSD_HPG = SSD_HEADS // SSD_GROUPS

DIFF_HEADS = 8
DIFF_HEAD_DIM = 128
DIFF_V = 256

MEM_HEADS = 4
MEM_HEAD_DIM = 128
MEM_WIDTH = MEM_HEADS * MEM_HEAD_DIM

PEER_HEADS = 8
PEER_KEYS = 128
PEER_TOPK = 16
PEER_HALF = 128

IN_XBC = 0
IN_Z = IN_XBC + SSD_CONV_DIM
IN_CQ = IN_Z + SSD_INNER
IN_CKV = IN_CQ + MLA_Q_RANK
IN_KR = IN_CKV + MLA_KV_RANK
IN_DT = IN_KR + LANES
IN_TOTAL = IN_DT + LANES

NEG_BIG = -1e30


def _cparams(sem, vmem_mb):
    return pltpu.CompilerParams(dimension_semantics=sem, vmem_limit_bytes=vmem_mb * 1024 * 1024)


def _rms(x, g, n=None):
    ss = jnp.sum(x * x, axis=-1, keepdims=True)
    n = x.shape[-1] if n is None else n
    return x * lax.rsqrt(ss * (1.0 / n) + EPS) * g


def _dot_nt(a, b):
    return lax.dot_general(a, b, (((1,), (1,)), ((), ())), preferred_element_type=F32)


def _norm_mm_kernel(x_ref, g_ref, w_ref, o_ref, xn_ref):
    @pl.when(pl.program_id(1) == 0)
    def _():
        xn_ref[...] = _rms(x_ref[...], g_ref[...]).astype(BF16)

    o_ref[...] = jnp.dot(xn_ref[...], w_ref[...], preferred_element_type=F32).astype(o_ref.dtype)


def norm_matmul(x, g, w, tm, tn, out_dtype=F32):
    t, d = x.shape
    n = w.shape[1]
    return pl.pallas_call(
        _norm_mm_kernel,
        grid=(t // tm, n // tn),
        in_specs=[pl.BlockSpec((tm, d), lambda i, j: (i, 0)),
                  pl.BlockSpec((1, d), lambda i, j: (0, 0)),
                  pl.BlockSpec((d, tn), lambda i, j: (0, j))],
        out_specs=pl.BlockSpec((tm, tn), lambda i, j: (i, j)),
        out_shape=jax.ShapeDtypeStruct((t, n), out_dtype),
        scratch_shapes=[pltpu.VMEM((tm, d), BF16)],
        compiler_params=_cparams(("parallel", "arbitrary"), 56),
        name="norm_matmul",
    )(x, g.reshape(1, d), w)


def _mm_res_kernel(*refs, n_a):
    res_ref, o_ref = refs[2 * n_a], refs[2 * n_a + 1]
    acc = res_ref[...]
    for a_ref, w_ref in zip(refs[:n_a], refs[n_a:2 * n_a]):
        acc = acc + jnp.dot(a_ref[...], w_ref[...], preferred_element_type=F32)
    o_ref[...] = acc


def matmul_residual(a_list, w_list, res, tm, tn):
    t, n = res.shape
    n_a = len(a_list)
    in_specs = ([pl.BlockSpec((tm, a.shape[1]), lambda i, j: (i, 0)) for a in a_list]
                + [pl.BlockSpec((w.shape[0], tn), lambda i, j: (0, j)) for w in w_list]
                + [pl.BlockSpec((tm, tn), lambda i, j: (i, j))])
    return pl.pallas_call(
        functools.partial(_mm_res_kernel, n_a=n_a),
        grid=(t // tm, n // tn),
        in_specs=in_specs,
        out_specs=pl.BlockSpec((tm, tn), lambda i, j: (i, j)),
        out_shape=jax.ShapeDtypeStruct((t, n), F32),
        compiler_params=_cparams(("parallel", "parallel"), 56),
        name="matmul_residual",
    )(*a_list, *w_list, res)


def _cast_kernel(x_ref, o_ref):
    o_ref[...] = x_ref[...].astype(o_ref.dtype)


def _cast_t_kernel(x_ref, o_ref):
    o_ref[...] = x_ref[...].T.astype(o_ref.dtype)


def cast_bf16(x, layer, tr):
    _, r, c = x.shape
    return pl.pallas_call(
        _cast_kernel,
        grid=(r // tr,),
        in_specs=[pl.BlockSpec((None, tr, c), lambda i: (layer, i, 0))],
        out_specs=pl.BlockSpec((tr, c), lambda i: (i, 0)),
        out_shape=jax.ShapeDtypeStruct((r, c), BF16),
        compiler_params=_cparams(("parallel",), 32),
        name="cast_bf16",
    )(x)


def cast_transpose_bf16(x, layer, tr):
    _, r, c = x.shape
    return pl.pallas_call(
        _cast_t_kernel,
        grid=(r // tr,),
        in_specs=[pl.BlockSpec((None, tr, c), lambda i: (layer, i, 0))],
        out_specs=pl.BlockSpec((None, c, tr), lambda i: (i, 0, 0)),
        out_shape=jax.ShapeDtypeStruct((r // tr, c, tr), BF16),
        compiler_params=_cparams(("parallel",), 32),
        name="cast_transpose_bf16",
    )(x)


def _rope_tab_kernel(pos_ref, inv_ref, sgn_ref, cos_ref, sin_ref):
    ang = pos_ref[...].astype(F32) * inv_ref[...]
    cos_ref[...] = jnp.cos(ang)
    sin_ref[...] = jnp.sin(ang) * sgn_ref[...]


def rope_tables(pos, half, stride):
    t = pos.shape[0]
    lane = np.arange(LANES)
    j = lane % stride
    inv = np.where(j < half, 1.0 / (ROPE_THETA ** (np.minimum(j, half - 1).astype(np.float32) / half)), 0.0)
    sgn = np.where(lane < stride, -1.0, 1.0)
    tm = min(t, 1024)
    return pl.pallas_call(
        _rope_tab_kernel,
        grid=(t // tm,),
        in_specs=[pl.BlockSpec((tm, 1), lambda i: (i, 0)),
                  pl.BlockSpec((1, LANES), lambda i: (0, 0)),
                  pl.BlockSpec((1, LANES), lambda i: (0, 0))],
        out_specs=[pl.BlockSpec((tm, LANES), lambda i: (i, 0))] * 2,
        out_shape=[jax.ShapeDtypeStruct((t, LANES), F32)] * 2,
        compiler_params=_cparams(("parallel",), 32),
        name="rope_tables",
    )(pos.reshape(t, 1), jnp.asarray(inv, F32).reshape(1, LANES), jnp.asarray(sgn, F32).reshape(1, LANES))


def _rope_seg(x, cos, sin):
    return x * cos + pltpu.roll(x, LANES // 2, 1) * sin


def _rep(x, n):
    return x if n == 1 else jnp.concatenate([x] * n, axis=1)


def _flash(chains, n_full, tq, tk):
    for (_, _, _, _, _, _, _, m_ref, l_ref, acc_ref) in chains:
        m_ref[...] = jnp.full(m_ref.shape, NEG_BIG, F32)
        l_ref[...] = jnp.zeros(l_ref.shape, F32)
        acc_ref[...] = jnp.zeros(acc_ref.shape, F32)

    def step(j, masked):
        start = pl.multiple_of(j * tk, tk)
        if masked:
            row = lax.broadcasted_iota(jnp.int32, (tq, tk), 0)
            col = lax.broadcasted_iota(jnp.int32, (tq, tk), 1)
            keep = col <= row
        for (q, k_ref, koff, dk, v_ref, voff, dv, m_ref, l_ref, acc_ref) in chains:
            s = _dot_nt(q, k_ref[pl.ds(start, tk), koff:koff + dk])
            if masked:
                s = jnp.where(keep, s, NEG_BIG)
            m_prev = m_ref[...]
            m_new = jnp.maximum(m_prev, jnp.max(s, axis=-1, keepdims=True))
            p = jnp.exp(s - _rep(m_new, tk // LANES))
            alpha = jnp.exp(m_prev - m_new)
            l_ref[...] = alpha * l_ref[...] + jnp.sum(p, axis=-1, keepdims=True)
            acc_ref[...] = (_rep(alpha, dv // LANES) * acc_ref[...]
                            + jnp.dot(p.astype(BF16), v_ref[pl.ds(start, tk), voff:voff + dv],
                                      preferred_element_type=F32))
            m_ref[...] = m_new

    def body(j, carry):
        step(j, False)
        return carry

    lax.fori_loop(0, n_full, body, 0)
    step(n_full, True)
    return [acc_ref[...] / _rep(l_ref[...], dv // LANES) for (_, _, _, _, _, _, dv, _, l_ref, acc_ref) in chains]


def _mla_prep_kernel(cq_ref, ckv_ref, kr_ref, qn_ref, kvn_ref, wuq_ref, wukv_ref, qg_ref, kg_ref,
                     cos_ref, sin_ref, q_out, k_out, v_out):
    cq = _rms(cq_ref[...], qn_ref[...]).astype(BF16)
    qf = jnp.dot(cq, wuq_ref[...], preferred_element_type=F32)
    ckv = _rms(ckv_ref[...], kvn_ref[...]).astype(BF16)
    kvf = jnp.dot(ckv, wukv_ref[...], preferred_element_type=F32)
    kr = kr_ref[...]
    kr_ss = jnp.sum(kr * kr, axis=-1, keepdims=True)
    cos, sin = cos_ref[...], sin_ref[...]
    qg, kg = qg_ref[...], kg_ref[...]
    scale = MLA_QK ** -0.5
    for h in range(MLA_HEADS):
        lo = h * MLA_QK_PAD
        qh = _rms(qf[:, lo:lo + MLA_QK_PAD], qg, MLA_QK) * scale
        q_out[:, lo:lo + LANES] = qh[:, :LANES].astype(BF16)
        q_out[:, lo + LANES:lo + 2 * LANES] = _rope_seg(qh[:, LANES:], cos, sin).astype(BF16)
        kn = kvf[:, lo:lo + MLA_NOPE]
        r = lax.rsqrt((jnp.sum(kn * kn, axis=-1, keepdims=True) + kr_ss) * (1.0 / MLA_QK) + EPS)
        k_out[:, lo:lo + LANES] = (kn * r * kg[:, :LANES]).astype(BF16)
        k_out[:, lo + LANES:lo + 2 * LANES] = _rope_seg(kr * r * kg[:, LANES:], cos, sin).astype(BF16)
        v_out[:, h * MLA_V:(h + 1) * MLA_V] = kvf[:, lo + MLA_NOPE:lo + MLA_QK_PAD].astype(BF16)


def _pad_rope_cols(w):
    half = MLA_ROPE // 2
    z = jnp.zeros(w.shape[:-1] + (half,), w.dtype)
    return jnp.concatenate([w[..., :half], z, w[..., half:], z], axis=-1)


def mla_prep(proj, q_norm, kv_norm, w_uq, w_ukv, q_gain, k_gain, cos, sin, tm):
    t = proj.shape[0]
    hq = MLA_HEADS * MLA_QK_PAD
    wq = w_uq.reshape(MLA_Q_RANK, MLA_HEADS, MLA_QK)
    wq = jnp.concatenate([wq[..., :MLA_NOPE], _pad_rope_cols(wq[..., MLA_NOPE:])], axis=-1)
    wq = wq.reshape(MLA_Q_RANK, hq).astype(BF16)
    qg = jnp.concatenate([q_gain[:MLA_NOPE], _pad_rope_cols(q_gain[MLA_NOPE:])]).reshape(1, MLA_QK_PAD)
    kg = jnp.concatenate([k_gain[:MLA_NOPE], _pad_rope_cols(k_gain[MLA_NOPE:])]).reshape(1, MLA_QK_PAD)
    const = lambda shape: pl.BlockSpec(shape, lambda i: (0, 0))
    return pl.pallas_call(
        _mla_prep_kernel,
        grid=(t // tm,),
        in_specs=[pl.BlockSpec((tm, MLA_Q_RANK), lambda i: (i, IN_CQ // MLA_Q_RANK)),
                  pl.BlockSpec((tm, MLA_KV_RANK), lambda i: (i, IN_CKV // MLA_KV_RANK)),
                  pl.BlockSpec((tm, LANES), lambda i: (i, IN_KR // LANES)),
                  const((1, MLA_Q_RANK)), const((1, MLA_KV_RANK)),
                  const((MLA_Q_RANK, hq)), const((MLA_KV_RANK, hq)),
                  const((1, MLA_QK_PAD)), const((1, MLA_QK_PAD)),
                  pl.BlockSpec((tm, LANES), lambda i: (i, 0)),
                  pl.BlockSpec((tm, LANES), lambda i: (i, 0))],
        out_specs=[pl.BlockSpec((tm, hq), lambda i: (i, 0)),
                   pl.BlockSpec((tm, hq), lambda i: (i, 0)),
                   pl.BlockSpec((tm, MLA_HEADS * MLA_V), lambda i: (i, 0))],
        out_shape=[jax.ShapeDtypeStruct((t, hq), BF16), jax.ShapeDtypeStruct((t, hq), BF16),
                   jax.ShapeDtypeStruct((t, MLA_HEADS * MLA_V), BF16)],
        compiler_params=_cparams(("parallel",), 56),
        name="mla_prep",
    )(proj, proj, proj, q_norm.reshape(1, -1), kv_norm.reshape(1, -1), wq, w_ukv.astype(BF16), qg, kg, cos, sin)


MLA_HPS = 4
DIFF_HPS = 4


def _mla_attn_kernel(q_ref, k_ref, v_ref, o_ref, m_ref, l_ref, acc_ref, *, tq):
    chains = [(q_ref[:, h * MLA_QK_PAD:(h + 1) * MLA_QK_PAD], k_ref, h * MLA_QK_PAD, MLA_QK_PAD,
               v_ref, h * MLA_V, MLA_V, m_ref.at[h], l_ref.at[h], acc_ref.at[h]) for h in range(MLA_HPS)]
    outs = _flash(chains, pl.program_id(2), tq, tq)
    for h in range(MLA_HPS):
        o_ref[:, h * MLA_V:(h + 1) * MLA_V] = outs[h].astype(o_ref.dtype)


def mla_attention(q, k, v, bsz, s, tq):
    hq = MLA_HEADS * MLA_QK_PAD
    q, k = q.reshape(bsz, s, hq), k.reshape(bsz, s, hq)
    v = v.reshape(bsz, s, MLA_HEADS * MLA_V)
    hps = MLA_HPS
    out = pl.pallas_call(
        functools.partial(_mla_attn_kernel, tq=tq),
        grid=(bsz, MLA_HEADS // hps, s // tq),
        in_specs=[pl.BlockSpec((None, tq, hps * MLA_QK_PAD), lambda b, h, i: (b, i, h)),
                  pl.BlockSpec((None, s, hps * MLA_QK_PAD), lambda b, h, i: (b, 0, h)),
                  pl.BlockSpec((None, s, hps * MLA_V), lambda b, h, i: (b, 0, h))],
        out_specs=pl.BlockSpec((None, tq, hps * MLA_V), lambda b, h, i: (b, i, h)),
        out_shape=jax.ShapeDtypeStruct((bsz, s, MLA_HEADS * MLA_V), BF16),
        scratch_shapes=[pltpu.VMEM((hps, tq, LANES), F32), pltpu.VMEM((hps, tq, LANES), F32),
                        pltpu.VMEM((hps, tq, MLA_V), F32)],
        compiler_params=_cparams(("parallel", "parallel", "arbitrary"), 48),
        name="mla_attention",
    )(q, k, v)
    return out.reshape(bsz * s, MLA_HEADS * MLA_V)


def _silu(x):
    return x * (1.0 / (1.0 + jnp.exp(-x)))


def _softplus(x):
    return jnp.maximum(x, 0.0) + jnp.log(1.0 + jnp.exp(-jnp.abs(x)))


def _ssd_kernel(xbc_ref, z_ref, dt_ref, cw_ref, cb_ref, dtb_ref, alog_ref, dsk_ref, nrm_ref, y_ref,
                xc_ref, st_ref):
    c = pl.program_id(1)
    L, P, N, G, HPG = SSD_CHUNK, SSD_HEAD_DIM, SSD_STATE, SSD_GROUPS, SSD_HPG
    TAIL = 8

    @pl.when(c == 0)
    def _():
        xc_ref[0:TAIL, :] = jnp.zeros((TAIL, SSD_CONV_DIM), F32)
        st_ref[...] = jnp.zeros(st_ref.shape, F32)

    xc_ref[TAIL:TAIL + L, :] = xbc_ref[...]
    conv = cb_ref[...] + jnp.zeros((L, SSD_CONV_DIM), F32)
    for k in range(SSD_CONV):
        off = TAIL - (SSD_CONV - 1) + k
        conv = conv + xc_ref[off:off + L, :] * cw_ref[k:k + 1, :]
    xc_ref[0:TAIL, :] = xc_ref[L:L + TAIL, :]
    xbc = _silu(conv)

    dt = _softplus(dt_ref[...] + dtb_ref[...])
    adt = dt * (-jnp.exp(alog_ref[...]))
    li = lax.broadcasted_iota(jnp.int32, (L, L), 0)
    si = lax.broadcasted_iota(jnp.int32, (L, L), 1)
    tri = (si <= li).astype(F32)
    a_cs = jnp.dot(tri, adt, preferred_element_type=F32, precision=lax.Precision.HIGHEST)
    a_cs_t = a_cs.T
    a_last = a_cs[L - 1:L, :]
    dec_state = jnp.exp(a_last - a_cs)
    dec_out = jnp.exp(a_cs)
    dec_chunk = jnp.exp(a_last)
    causal = si <= li

    zs = _silu(z_ref[...])
    for g in range(G):
        bg = xbc[:, SSD_INNER + g * N:SSD_INNER + (g + 1) * N]
        cg = xbc[:, SSD_INNER + G * N + g * N:SSD_INNER + G * N + (g + 1) * N]
        bg16, cg16 = bg.astype(BF16), cg.astype(BF16)
        cb = _dot_nt(cg16, bg16)
        bgt16 = bg.T.astype(BF16)
        xs_g = xbc[:, g * HPG * P:(g + 1) * HPG * P]
        xdd, ydiag = [], []
        for hh in range(HPG):
            h = g * HPG + hh
            xs = xs_g[:, hh * P:(hh + 1) * P]
            xd = xs * dt[:, h:h + 1]
            seg = a_cs[:, h:h + 1] - a_cs_t[h:h + 1, :]
            lm = cb * jnp.exp(jnp.where(causal, seg, NEG_BIG))
            ydiag.append(jnp.dot(lm.astype(BF16), xd.astype(BF16), preferred_element_type=F32))
            xdd.append((xd * dec_state[:, h:h + 1]).astype(BF16))
        prev = st_ref[g]
        yoff = jnp.dot(cg16, prev.astype(BF16), preferred_element_type=F32)
        new = jnp.dot(bgt16, jnp.concatenate(xdd, axis=1), preferred_element_type=F32)
        ys, decs = [], []
        for hh in range(HPG):
            h = g * HPG + hh
            xs = xs_g[:, hh * P:(hh + 1) * P]
            y = ydiag[hh] + yoff[:, hh * P:(hh + 1) * P] * dec_out[:, h:h + 1] + dsk_ref[:, h:h + 1] * xs
            ys.append(y)
            decs.append(jnp.broadcast_to(dec_chunk[:, h:h + 1], (N, P)))
        st_ref[g] = prev * jnp.concatenate(decs, axis=1) + new
        yg = jnp.concatenate(ys, axis=1) * zs[:, g * HPG * P:(g + 1) * HPG * P]
        y_ref[:, g * HPG * P:(g + 1) * HPG * P] = _rms(yg, nrm_ref[:, g * HPG * P:(g + 1) * HPG * P]).astype(y_ref.dtype)


def _pad_lanes(v):
    return jnp.pad(v.astype(F32), (0, LANES - v.shape[0])).reshape(1, LANES)


def ssd_mixer(proj, conv_w, conv_b, dt_bias, a_log, d_skip, ssd_norm, bsz, s):
    nc = s // SSD_CHUNK
    const = lambda shape: pl.BlockSpec(shape, lambda b, c: (0, 0))
    return pl.pallas_call(
        _ssd_kernel,
        grid=(bsz, nc),
        in_specs=[pl.BlockSpec((SSD_CHUNK, SSD_CONV_DIM), lambda b, c: (b * nc + c, IN_XBC // SSD_CONV_DIM)),
                  pl.BlockSpec((SSD_CHUNK, SSD_INNER), lambda b, c: (b * nc + c, IN_Z // SSD_INNER)),
                  pl.BlockSpec((SSD_CHUNK, LANES), lambda b, c: (b * nc + c, IN_DT // LANES)),
                  const((SSD_CONV, SSD_CONV_DIM)), const((1, SSD_CONV_DIM)),
                  const((1, LANES)), const((1, LANES)), const((1, LANES)), const((1, SSD_INNER))],
        out_specs=pl.BlockSpec((SSD_CHUNK, SSD_INNER), lambda b, c: (b * nc + c, 0)),
        out_shape=jax.ShapeDtypeStruct((bsz * s, SSD_INNER), BF16),
        scratch_shapes=[pltpu.VMEM((SSD_CHUNK + 8, SSD_CONV_DIM), F32),
                        pltpu.VMEM((SSD_GROUPS, SSD_STATE, SSD_HPG * SSD_HEAD_DIM), F32)],
        compiler_params=_cparams(("parallel", "arbitrary"), 48),
        name="ssd_mixer",
    )(proj, proj, proj, conv_w.astype(F32), conv_b.reshape(1, -1).astype(F32),
      _pad_lanes(dt_bias), _pad_lanes(a_log), _pad_lanes(d_skip), ssd_norm.reshape(1, -1).astype(F32))


def _diff_qkv_kernel(x_ref, g_ref, w_ref, qg_ref, kg_ref, cos_ref, sin_ref, q_out, k_out, v_out, xn_ref, raw_ref):
    j = pl.program_id(1)

    def normed(gain_ref, scale, out):
        cos, sin = cos_ref[...], sin_ref[...]
        for m in range(2 * DIFF_HEADS):
            lo = m * DIFF_HEAD_DIM
            y = _rms(raw_ref[:, lo:lo + DIFF_HEAD_DIM], gain_ref[...]) * scale
            out[:, lo:lo + DIFF_HEAD_DIM] = _rope_seg(y, cos, sin).astype(BF16)

    @pl.when(j == 0)
    def _():
        xn_ref[...] = _rms(x_ref[...], g_ref[...]).astype(BF16)
        raw_ref[...] = jnp.dot(xn_ref[...], w_ref[...], preferred_element_type=F32)

    @pl.when(j == 1)
    def _():
        acc = jnp.dot(xn_ref[...], w_ref[...], preferred_element_type=F32)
        normed(qg_ref, DIFF_HEAD_DIM ** -0.5, q_out)
        raw_ref[...] = acc

    @pl.when(j == 2)
    def _():
        v_out[...] = jnp.dot(xn_ref[...], w_ref[...], preferred_element_type=F32).astype(BF16)
        normed(kg_ref, 1.0, k_out)


def diff_qkv(x, norm_g, w_qkv, q_gain, k_gain, cos, sin, tm):
    t, dm = x.shape
    d = 2 * DIFF_HEADS * DIFF_HEAD_DIM
    const = lambda shape: pl.BlockSpec(shape, lambda i, j: (0, 0))
    return pl.pallas_call(
        _diff_qkv_kernel,
        grid=(t // tm, 3),
        in_specs=[pl.BlockSpec((tm, dm), lambda i, j: (i, 0)),
                  const((1, dm)),
                  pl.BlockSpec((dm, d), lambda i, j: (0, j)),
                  const((1, DIFF_HEAD_DIM)), const((1, DIFF_HEAD_DIM)),
                  pl.BlockSpec((tm, LANES), lambda i, j: (i, 0)),
                  pl.BlockSpec((tm, LANES), lambda i, j: (i, 0))],
        out_specs=[pl.BlockSpec((tm, d), lambda i, j: (i, 0))] * 3,
        out_shape=[jax.ShapeDtypeStruct((t, d), BF16)] * 3,
        scratch_shapes=[pltpu.VMEM((tm, dm), BF16), pltpu.VMEM((tm, d), F32)],
        compiler_params=_cparams(("parallel", "arbitrary"), 56),
        name="diff_qkv",
    )(x, norm_g.reshape(1, dm), w_qkv, q_gain.reshape(1, -1), k_gain.reshape(1, -1), cos, sin)


def _diff_attn_kernel(q_ref, k_ref, v_ref, lq1_ref, lk1_ref, lq2_ref, lk2_ref, sub_ref, o_ref,
                      m_ref, l_ref, acc_ref, *, tq, lam_init):
    dh = DIFF_HEAD_DIM
    chains = [(q_ref[:, c * dh:(c + 1) * dh], k_ref, c * dh, dh, v_ref, (c // 2) * DIFF_V, DIFF_V,
               m_ref.at[c], l_ref.at[c], acc_ref.at[c]) for c in range(2 * DIFF_HPS)]
    outs = _flash(chains, pl.program_id(2), tq, tq)
    lam = (jnp.exp(jnp.sum(lq1_ref[...] * lk1_ref[...], axis=-1, keepdims=True))
           - jnp.exp(jnp.sum(lq2_ref[...] * lk2_ref[...], axis=-1, keepdims=True)) + lam_init)
    for h in range(DIFF_HPS):
        attn = outs[2 * h] - lam * outs[2 * h + 1]
        o_ref[:, h * DIFF_V:(h + 1) * DIFF_V] = (_rms(attn, sub_ref[...]) * (1.0 - lam_init)).astype(o_ref.dtype)


def diff_attention(q, k, v, lam_q1, lam_k1, lam_q2, lam_k2, subln, lam_init, bsz, s, tq):
    d = 2 * DIFF_HEADS * DIFF_HEAD_DIM
    q, k, v = q.reshape(bsz, s, d), k.reshape(bsz, s, d), v.reshape(bsz, s, d)
    hps = DIFF_HPS
    const = lambda shape: pl.BlockSpec(shape, lambda b, h, i: (0, 0))
    out = pl.pallas_call(
        functools.partial(_diff_attn_kernel, tq=tq, lam_init=lam_init),
        grid=(bsz, DIFF_HEADS // hps, s // tq),
        in_specs=[pl.BlockSpec((None, tq, hps * DIFF_V), lambda b, h, i: (b, i, h)),
                  pl.BlockSpec((None, s, hps * DIFF_V), lambda b, h, i: (b, 0, h)),
                  pl.BlockSpec((None, s, hps * DIFF_V), lambda b, h, i: (b, 0, h)),
                  const((1, DIFF_HEAD_DIM)), const((1, DIFF_HEAD_DIM)),
                  const((1, DIFF_HEAD_DIM)), const((1, DIFF_HEAD_DIM)), const((1, DIFF_V))],
        out_specs=pl.BlockSpec((None, tq, hps * DIFF_V), lambda b, h, i: (b, i, h)),
        out_shape=jax.ShapeDtypeStruct((bsz, s, d), BF16),
        scratch_shapes=[pltpu.VMEM((2 * hps, tq, LANES), F32), pltpu.VMEM((2 * hps, tq, LANES), F32),
                        pltpu.VMEM((2 * hps, tq, DIFF_V), F32)],
        compiler_params=_cparams(("parallel", "parallel", "arbitrary"), 56),
        name="diff_attention",
    )(q, k, v, lam_q1.reshape(1, -1), lam_k1.reshape(1, -1), lam_q2.reshape(1, -1), lam_k2.reshape(1, -1),
      subln.reshape(1, -1))
    return out.reshape(bsz * s, d)


def _mem_attn_kernel(x_ref, g_ref, wq_ref, kv_ref, qg_ref, kg_ref, wo_ref, o_ref):
    x = x_ref[...]
    xn = _rms(x, g_ref[...]).astype(BF16)
    q = jnp.dot(xn, wq_ref[...], preferred_element_type=F32)
    kv = kv_ref[...]
    outs = []
    for h in range(MEM_HEADS):
        lo = h * MEM_HEAD_DIM
        qh = (_rms(q[:, lo:lo + MEM_HEAD_DIM], qg_ref[...]) * (MEM_HEAD_DIM ** -0.5)).astype(BF16)
        kh = _rms(kv[:, lo:lo + MEM_HEAD_DIM], kg_ref[...]).astype(BF16)
        vh = kv[:, MEM_WIDTH + lo:MEM_WIDTH + lo + MEM_HEAD_DIM].astype(BF16)
        s = _dot_nt(qh, kh)
        p = jnp.exp(s - jnp.max(s, axis=-1, keepdims=True))
        o = jnp.dot(p.astype(BF16), vh, preferred_element_type=F32)
        outs.append((o / jnp.sum(p, axis=-1, keepdims=True)).astype(BF16))
    o_ref[...] = x + jnp.dot(jnp.concatenate(outs, axis=1), wo_ref[...], preferred_element_type=F32)


def memory_attention(x, mem_kv, norm_g, w_q, q_gain, k_gain, w_o, bsz, s, tm):
    t, d = x.shape
    m = mem_kv.shape[0] // bsz
    nb = s // tm
    const = lambda shape: pl.BlockSpec(shape, lambda i: (0, 0))
    return pl.pallas_call(
        _mem_attn_kernel,
        grid=(t // tm,),
        in_specs=[pl.BlockSpec((tm, d), lambda i: (i, 0)),
                  const((1, d)), const((d, MEM_WIDTH)),
                  pl.BlockSpec((m, 2 * MEM_WIDTH), lambda i: (i // nb, 0)),
                  const((1, MEM_HEAD_DIM)), const((1, MEM_HEAD_DIM)), const((MEM_WIDTH, d))],
        out_specs=pl.BlockSpec((tm, d), lambda i: (i, 0)),
        out_shape=jax.ShapeDtypeStruct((t, d), F32),
        compiler_params=_cparams(("parallel",), 56),
        name="memory_attention",
    )(x, norm_g.reshape(1, d), w_q.astype(BF16), mem_kv, q_gain.reshape(1, -1), k_gain.reshape(1, -1),
      w_o.astype(BF16))


def _compare_exchange(lst, i, j):
    a, b = lst[i], lst[j]
    if b is None:
        return
    if a is None:
        lst[i], lst[j] = b, None
        return
    lst[i], lst[j] = jnp.maximum(a, b), jnp.minimum(a, b)


def _bitonic_sort(lst):
    n = len(lst)
    k = 2
    while k <= n:
        j = k // 2
        while j >= 1:
            for i in range(n):
                l = i ^ j
                if l > i:
                    if i & k == 0:
                        _compare_exchange(lst, i, l)
                    else:
                        _compare_exchange(lst, l, i)
            j //= 2
        k *= 2


def _bitonic_merge(lst):
    j = len(lst) // 2
    while j >= 1:
        for i in range(len(lst)):
            if i ^ j > i:
                _compare_exchange(lst, i, i ^ j)
        j //= 2


def _top_values(x, k):
    sub = 8
    nb = x.shape[0] // sub
    size = 1
    while size < nb:
        size *= 2
    lst = [x[i * sub:(i + 1) * sub, :] for i in range(nb)] + [None] * (size - nb)
    _bitonic_sort(lst)
    lst = (lst + [None] * k)[:k]
    shift = sub // 2
    while shift >= 1:
        other = [None if v is None else pltpu.roll(v, shift, 0) for v in lst]
        merged = []
        for i in range(k):
            a, b = lst[i], other[k - 1 - i]
            merged.append(b if a is None else a if b is None else jnp.maximum(a, b))
        _bitonic_merge(merged)
        lst = merged
        shift //= 2
    return [v[0:1, :] for v in lst]


_PAIRS = [(i, j) for i in range(PEER_TOPK) for j in range(PEER_TOPK) if (i + 1) * (j + 1) <= PEER_TOPK]
_PAIR_ROWS = -(-len(_PAIRS) // 8) * 8


def _partner_threshold(s0, b, tau):
    hit = lambda v: s0 + v >= tau
    m8 = hit(b[7])
    thr = jnp.where(m8, b[7], jnp.inf)
    b4 = jnp.where(m8, b[11], b[3])
    m4 = hit(b4)
    thr = jnp.where(m4, b4, thr)
    b2 = jnp.where(m8, jnp.where(m4, b[13], b[9]), jnp.where(m4, b[5], b[1]))
    m2 = hit(b2)
    thr = jnp.where(m2, b2, thr)
    b1 = jnp.where(m8, jnp.where(m4, jnp.where(m2, b[14], b[12]), jnp.where(m2, b[10], b[8])),
                   jnp.where(m4, jnp.where(m2, b[6], b[4]), jnp.where(m2, b[2], b[0])))
    thr = jnp.where(hit(b1), b1, thr)
    return jnp.where(hit(b[15]), b[15], thr)


def _peer_route_kernel(x_ref, g_ref, wq_ref, sk_ref, xn_out, thr_out, e0_out, s1_out, e1_out, sc_ref):
    tm = x_ref.shape[0]
    K = PEER_KEYS
    RB = 64
    xn32 = _rms(x_ref[...], g_ref[...])
    xn_out[...] = xn32.T.astype(BF16)
    q = jnp.dot(xn32.astype(BF16), wq_ref[...], preferred_element_type=F32).astype(BF16)
    for g in range(2 * PEER_HEADS):
        sc_ref[g * K:(g + 1) * K, :] = _dot_nt(sk_ref[g], q[:, g * PEER_HALF:(g + 1) * PEER_HALF])
    for h in range(PEER_HEADS):
        lo0, lo1 = 2 * h * K, (2 * h + 1) * K
        a = _top_values(sc_ref[lo0:lo0 + K, :], PEER_TOPK)
        b = _top_values(sc_ref[lo1:lo1 + K, :], PEER_TOPK)
        cands = [a[i] + b[j] for i, j in _PAIRS]
        cands += [jnp.full((1, tm), -jnp.inf, F32)] * (_PAIR_ROWS - len(_PAIRS))
        cand = jnp.concatenate(cands, axis=0)
        tau = _top_values(cand, PEER_TOPK)[-1]
        z = jnp.sum(jnp.where(cand >= tau, jnp.exp(cand - (a[0] + b[0])), 0.0), axis=0, keepdims=True)
        for c in range(tm // LANES):
            cs = slice(c * LANES, (c + 1) * LANES)
            for rb in range(K // RB):
                s0 = sc_ref[lo0 + rb * RB:lo0 + (rb + 1) * RB, cs]
                thr_out[h * K + rb * RB:h * K + (rb + 1) * RB, cs] = _partner_threshold(
                    s0, [v[:, cs] for v in b], tau[:, cs])
        e0_out[h * K:(h + 1) * K, :] = jnp.exp(sc_ref[lo0:lo0 + K, :] - a[0])
        s1_out[h * K:(h + 1) * K, :] = sc_ref[lo1:lo1 + K, :]
        e1_out[h * K:(h + 1) * K, :] = jnp.exp(sc_ref[lo1:lo1 + K, :] - b[0]) * (1.0 / z)


def peer_route(x, norm_g, w_q, sub_keys, tm):
    t, d = x.shape
    nq = 2 * PEER_HEADS * PEER_HALF
    nk = PEER_HEADS * PEER_KEYS
    sk = sub_keys.reshape(2 * PEER_HEADS, PEER_KEYS, PEER_HALF).astype(BF16)
    col = lambda rows: pl.BlockSpec((None, rows, tm), lambda i: (i, 0, 0))
    return pl.pallas_call(
        _peer_route_kernel,
        grid=(t // tm,),
        in_specs=[pl.BlockSpec((tm, d), lambda i: (i, 0)),
                  pl.BlockSpec((1, d), lambda i: (0, 0)),
                  pl.BlockSpec((d, nq), lambda i: (0, 0)),
                  pl.BlockSpec((2 * PEER_HEADS, PEER_KEYS, PEER_HALF), lambda i: (0, 0, 0))],
        out_specs=[col(d), col(nk), col(nk), col(nk), col(nk)],
        out_shape=[jax.ShapeDtypeStruct((t // tm, d, tm), BF16)] + [jax.ShapeDtypeStruct((t // tm, nk, tm), F32)] * 4,
        scratch_shapes=[pltpu.VMEM((2 * nk, tm), F32)],
        compiler_params=_cparams(("parallel",), 56),
        name="peer_route",
    )(x, norm_g.reshape(1, d), w_q.astype(BF16), sk)


def _gelu(x):
    return 0.5 * x * (1.0 + lax.erf(x * (2.0 ** -0.5)))


def _peer_gated_act(blk, half, js, nr, xu_ref, w_ref, thr_ref, e0_ref, s1_ref, e1_ref):
    K = PEER_KEYS
    tmh = xu_ref.shape[1]
    for j in js:
        r = jnp.minimum(blk * nr + j, K - 1)
        thr_rows = [thr_ref[pl.ds(h * K + r, 1), :] for h in range(PEER_HEADS)]
        e0_rows = [e0_ref[pl.ds(h * K + r, 1), :] for h in range(PEER_HEADS)]
        for c in range(tmh // LANES):
            cl = slice(c * LANES, (c + 1) * LANES)
            cs = slice(half * tmh + c * LANES, half * tmh + (c + 1) * LANES)
            gate = None
            for h in range(PEER_HEADS):
                part = jnp.where(s1_ref[h * K:(h + 1) * K, cs] >= thr_rows[h][:, cs],
                                 e1_ref[h * K:(h + 1) * K, cs] * e0_rows[h][:, cs], 0.0)
                gate = part if gate is None else gate + part
            w_ref[j * K:(j + 1) * K, cl] = (_gelu(xu_ref[j * K:(j + 1) * K, cl]) * gate).astype(BF16)


def _peer_dense_kernel(xnt_ref, thr_ref, e0_ref, s1_ref, e1_ref, u_ref, vta_ref, vtb_ref, o_ref,
                       xu0_ref, xu1_ref, w0_ref, w1_ref, *, nr):
    g = pl.program_id(1)
    last = pl.num_programs(1) - 1
    eb = nr * PEER_KEYS
    tmh = xnt_ref.shape[1] // 2
    xu, w = (xu0_ref, xu1_ref), (w0_ref, w1_ref)

    def score(s):
        hd = s % 2
        xu[hd][...] = jnp.dot(u_ref[(s // 2) * eb:(s // 2 + 1) * eb, :], xnt_ref[:, hd * tmh:(hd + 1) * tmh],
                              preferred_element_type=F32)

    def value(s):
        hd = s % 2
        vt_ref = vtb_ref if s < 2 else vta_ref
        o_ref[:, hd * tmh:(hd + 1) * tmh] += jnp.dot(vt_ref[...], w[hd][...], preferred_element_type=F32)

    def act(s, js):
        hv = (s + 1) % 2
        _peer_gated_act(2 * g + (s - 1) // 2, hv, js, nr, xu[hv], w[hv], thr_ref, e0_ref, s1_ref, e1_ref)

    def slot(s):
        score(s)
        act(s, range(0, nr // 2))
        value(s)
        act(s, range(nr // 2, nr))

    @pl.when(g == 0)
    def _():
        o_ref[...] = jnp.zeros(o_ref.shape, F32)
        score(0)
        score(1)
        act(1, range(nr))

    @pl.when(g > 0)
    def _():
        slot(0)
        slot(1)

    @pl.when(g < last)
    def _():
        slot(2)
        slot(3)


def peer_dense(xnt, thr, e0, s1, e1, u_tab, v_tab, layer, eb):
    nt, d, tm = xnt.shape
    n_exp = u_tab.shape[1]
    nblk = n_exp // eb
    nk = thr.shape[1]
    u16 = cast_bf16(u_tab, layer, 1024)
    vt16 = cast_transpose_bf16(v_tab, layer, eb)
    col = lambda rows: pl.BlockSpec((None, rows, tm), lambda i, g: (i, 0, 0))
    return pl.pallas_call(
        functools.partial(_peer_dense_kernel, nr=eb // PEER_KEYS),
        grid=(nt, nblk // 2 + 1),
        in_specs=[col(d), col(nk), col(nk), col(nk), col(nk),
                  pl.BlockSpec((2 * eb, d), lambda i, g: (jnp.minimum(g, nblk // 2 - 1), 0)),
                  pl.BlockSpec((None, d, eb), lambda i, g: (jnp.minimum(2 * g, nblk - 1), 0, 0)),
                  pl.BlockSpec((None, d, eb), lambda i, g: (jnp.maximum(2 * g - 1, 0), 0, 0))],
        out_specs=col(d),
        out_shape=jax.ShapeDtypeStruct((nt, d, tm), F32),
        scratch_shapes=[pltpu.VMEM((eb, tm // 2), F32),
                        pltpu.VMEM((eb, tm // 2), F32),
                        pltpu.VMEM((eb, tm // 2), BF16),
                        pltpu.VMEM((eb, tm // 2), BF16)],
        compiler_params=_cparams(("parallel", "arbitrary"), 56),
        name="peer_dense",
    )(xnt, thr, e0, s1, e1, u16, vt16, vt16)


def _transpose_add_kernel(x_ref, pt_ref, o_ref):
    o_ref[...] = x_ref[...] + pt_ref[...].T


def transpose_add(x, pt):
    t, d = x.shape
    tm = pt.shape[2]
    return pl.pallas_call(
        _transpose_add_kernel,
        grid=(t // tm,),
        in_specs=[pl.BlockSpec((tm, d), lambda i: (i, 0)), pl.BlockSpec((None, d, tm), lambda i: (i, 0, 0))],
        out_specs=pl.BlockSpec((tm, d), lambda i: (i, 0)),
        out_shape=jax.ShapeDtypeStruct((t, d), F32),
        compiler_params=_cparams(("parallel",), 48),
        name="transpose_add",
    )(x, pt)


def _even_w_in(w_in):
    o = np.cumsum([0, MLA_Q_RANK, MLA_KV_RANK, MLA_ROPE, SSD_INNER, SSD_CONV_DIM, SSD_HEADS])
    c_q, c_kv, k_r, z, xbc, dt = (w_in[:, o[i]:o[i + 1]] for i in range(6))
    dt = jnp.pad(dt, ((0, 0), (0, LANES - SSD_HEADS)))
    return jnp.concatenate([xbc, z, c_q, c_kv, _pad_rope_cols(k_r), dt], axis=1).astype(BF16)


def _tile(n, pref):
    return pref if n % pref == 0 else n


def kernel(x, mem, positions, norm_mix, norm_mem, norm_mem_kv, norm_ffn, mem_w_q, mem_w_kv, mem_q_gain, mem_k_gain, mem_w_o, peer_w_q, peer_sub_keys, peer_u, peer_v, ev_w_in, ev_q_norm, ev_kv_norm, ev_w_uq, ev_w_ukv, ev_q_gain, ev_k_gain, ev_conv_w, ev_conv_b, ev_dt_bias, ev_a_log, ev_d_skip, ev_ssd_norm, ev_w_out, od_w_qkv, od_q_gain, od_k_gain, od_lam_q1, od_lam_k1, od_lam_q2, od_lam_k2, od_subln, od_w_o):
    bsz, s, d = x.shape
    t = bsz * s
    depth = norm_mix.shape[0]
    tm = _tile(s, 512)
    tq = _tile(s, 512)
    tm2 = _tile(t, 2 * tm)
    xf = x.reshape(t, d)
    pos = positions.reshape(t)
    mem_f = mem.reshape(-1, d)
    cos_mla, sin_mla = rope_tables(pos, MLA_ROPE // 2, LANES // 2)
    cos_dif, sin_dif = rope_tables(pos, DIFF_HEAD_DIM // 2, LANES // 2)

    for layer in range(depth):
        i = layer // 2
        if layer % 2 == 0:
            proj = norm_matmul(xf, norm_mix[layer], _even_w_in(ev_w_in[i]), tm, IN_TOTAL // 2)
            q, k, v = mla_prep(proj, ev_q_norm[i], ev_kv_norm[i], ev_w_uq[i], ev_w_ukv[i],
                               ev_q_gain[i], ev_k_gain[i], cos_mla, sin_mla, tm)
            mla_out = mla_attention(q, k, v, bsz, s, tq)
            y = ssd_mixer(proj, ev_conv_w[i], ev_conv_b[i], ev_dt_bias[i], ev_a_log[i], ev_d_skip[i],
                          ev_ssd_norm[i], bsz, s)
            w_out = ev_w_out[i].astype(BF16)
            n_mla = MLA_HEADS * MLA_V
            xf = matmul_residual([mla_out, y], [w_out[:n_mla], w_out[n_mla:]], xf, tm2, 1024)
        else:
            q, k, v = diff_qkv(xf, norm_mix[layer], od_w_qkv[i].astype(BF16), od_q_gain[i], od_k_gain[i],
                               cos_dif, sin_dif, tm)
            lam_init = 0.8 - 0.6 * math.exp(-0.3 * layer)
            attn = diff_attention(q, k, v, od_lam_q1[i], od_lam_k1[i], od_lam_q2[i], od_lam_k2[i],
                                  od_subln[i], lam_init, bsz, s, tq)
            xf = matmul_residual([attn], [od_w_o[i].astype(BF16)], xf, tm2, 1024)
        mem_kv = norm_matmul(mem_f, norm_mem_kv[layer], mem_w_kv[layer].astype(BF16),
                             _tile(mem_f.shape[0], 512), 2 * MEM_WIDTH)
        xf = memory_attention(xf, mem_kv, norm_mem[layer], mem_w_q[layer], mem_q_gain[layer],
                              mem_k_gain[layer], mem_w_o[layer], bsz, s, tm)
        routing = peer_route(xf, norm_ffn[layer], peer_w_q[layer], peer_sub_keys[layer], tm)
        xf = transpose_add(xf, peer_dense(*routing, peer_u, peer_v, layer, 512))
    return xf.reshape(bsz, s, d)
```
